```python
import math
import jax, jax.numpy as jnp
from jax import lax
import numpy as np

D_MODEL = 2048
BATCH = 2
SEQ = 4096
DEPTH = 2
DEC_BATCH = 128
DEC_SEQ = 8
PAST_LEN = 16384
PAGE_SIZE = 128

MLA_HEADS = 8
MLA_NOPE = 128
MLA_ROPE = 64
MLA_VDIM = 128
MLA_Q_RANK = 512
MLA_KV_RANK = 256
DIFF_HEADS = 8
DIFF_HD = 64
DIFF_VD = 2 * DIFF_HD
MIX_WIDTH = MLA_HEADS * MLA_VDIM + DIFF_HEADS * DIFF_VD
D_FF = ((8 * D_MODEL // 3 + 127) // 128) * 128
N_BUCKETS = 32
MAX_DISTANCE = 128
ROPE_THETA = 10000.0
EPS = 1e-6
Q_BLOCK = 128
MLA_SCALE = (MLA_NOPE + MLA_ROPE) ** -0.5
DIFF_SCALE = DIFF_HD ** -0.5
IN_SIZES = (MLA_Q_RANK, MLA_KV_RANK, MLA_ROPE, DIFF_HEADS * 2 * DIFF_HD, 2 * DIFF_HD, DIFF_VD)
IN_COLS = sum(IN_SIZES)
IN_SPLITS = tuple(int(s) for s in np.cumsum(IN_SIZES)[:-1])

kernel_name = 'hymba_mla_diffattn_macaron_step'


def rmsnorm(x, g):
    xf = x.astype(jnp.float32)
    y = xf * lax.rsqrt(jnp.mean(xf * xf, axis=-1, keepdims=True) + EPS)
    return (y * g.astype(jnp.float32)).astype(x.dtype)


def rope_angles(pos):
    inv = ROPE_THETA ** (-jnp.arange(0, MLA_ROPE, 2, dtype=jnp.float32) / MLA_ROPE)
    ang = pos.astype(jnp.float32)[:, None] * inv[None, :]
    return jnp.cos(ang), jnp.sin(ang)


def apply_rope(x, cos, sin):
    extra = x.ndim - 3
    c = cos.reshape(cos.shape[:1] + (1,) * extra + cos.shape[1:])
    s = sin.reshape(sin.shape[:1] + (1,) * extra + sin.shape[1:])
    x1, x2 = jnp.split(x.astype(jnp.float32), 2, axis=-1)
    return jnp.concatenate([x1 * c - x2 * s, x1 * s + x2 * c], axis=-1).astype(x.dtype)


def t5_bucket(dist):
    n = jnp.maximum(dist, 0)
    max_exact = N_BUCKETS // 2
    nf = jnp.maximum(n, 1).astype(jnp.float32)
    large = max_exact + (jnp.log(nf / max_exact) / math.log(MAX_DISTANCE / max_exact)
                         * (N_BUCKETS - max_exact)).astype(jnp.int32)
    large = jnp.minimum(large, N_BUCKETS - 1)
    return jnp.where(n < max_exact, n, large)


def swiglu_half(x, g, wg, wu, wd):
    h = rmsnorm(x, g)
    return x + 0.5 * ((jax.nn.silu(h @ wg) * (h @ wu)) @ wd)


def project(h, cos, sin, w_in, g_cq, w_uq, g_qn, g_qr, g_ckv, g_kr, g_dq, g_dk):
    B, T, _ = h.shape
    cq, ckv, kpe, dq, dk, dv = jnp.split(h @ w_in, IN_SPLITS, axis=-1)
    q = (rmsnorm(cq, g_cq) @ w_uq).reshape(B, T, MLA_HEADS, MLA_NOPE + MLA_ROPE)
    q_nope = rmsnorm(q[..., :MLA_NOPE], g_qn)
    q_pe = apply_rope(rmsnorm(q[..., MLA_NOPE:], g_qr), cos, sin)
    ckv = rmsnorm(ckv, g_ckv)
    kpe = apply_rope(rmsnorm(kpe, g_kr), cos, sin)
    dq = rmsnorm(dq.reshape(B, T, DIFF_HEADS, 2, DIFF_HD), g_dq)
    dk = rmsnorm(dk.reshape(B, T, 2, DIFF_HD), g_dk).reshape(B, T, 2 * DIFF_HD)
    return q_nope, q_pe, ckv, kpe, dq[..., 0, :], dq[..., 1, :], dk, dv


def mla_expand(ckv, w_uk, w_uv, g_kn):
    k_nope = rmsnorm(jnp.einsum('btr,rhd->bthd', ckv, w_uk), g_kn)
    v = jnp.einsum('btr,rhd->bthd', ckv, w_uv)
    return k_nope, v


def mla_attend(q_nope, q_pe, k_nope, k_pe, v, qpos, kpos):
    s = (jnp.einsum('bqhd,bkhd->bhqk', q_nope, k_nope)
         + jnp.einsum('bqhd,bkd->bhqk', q_pe, k_pe)).astype(jnp.float32) * MLA_SCALE
    s = jnp.where(kpos[None, :] <= qpos[:, None], s, -jnp.inf)
    a = jax.nn.softmax(s, axis=-1).astype(v.dtype)
    return jnp.einsum('bhqk,bkhd->bqhd', a, v)


def diff_attend(q1, q2, k1, k2, v, lam, rel_bias, qpos, kpos):
    dist = qpos[:, None] - kpos[None, :]
    bias = jnp.transpose(rel_bias[t5_bucket(dist)], (2, 0, 1)).astype(jnp.float32)
    mask = dist >= 0

    def probs(q, k):
        s = jnp.einsum('bqhd,bkd->bhqk', q, k).astype(jnp.float32) * DIFF_SCALE + bias
        return jax.nn.softmax(jnp.where(mask, s, -jnp.inf), axis=-1)

    a = probs(q1, k1) - lam * probs(q2, k2)
    return jnp.einsum('bhqk,bkd->bqhd', a.astype(v.dtype), v)


def prompt_mixers(q_nope, q_pe, ckv, kpe, q1, q2, dk, dv, lam, rel_bias, w_uk, w_uv, g_kn):
    S = q_nope.shape[1]
    pos = jnp.arange(S, dtype=jnp.int32)
    k_nope, v = mla_expand(ckv, w_uk, w_uv, g_kn)
    k1, k2 = dk[..., :DIFF_HD], dk[..., DIFF_HD:]
    nblk = S // Q_BLOCK

    def to_blocks(a):
        return jnp.moveaxis(a.reshape((a.shape[0], nblk, Q_BLOCK) + a.shape[2:]), 1, 0)

    def from_blocks(a):
        a = jnp.moveaxis(a, 0, 1)
        return a.reshape((a.shape[0], S) + a.shape[3:])

    def block(args):
        qn, qp, a1, a2, qpos = args
        o_m = mla_attend(qn, qp, k_nope, kpe, v, qpos, pos)
        o_d = diff_attend(a1, a2, k1, k2, dv, lam, rel_bias, qpos, pos)
        return o_m, o_d

    o_m, o_d = lax.map(block, (to_blocks(q_nope), to_blocks(q_pe), to_blocks(q1), to_blocks(q2),
                               pos.reshape(nblk, Q_BLOCK)))
    return from_blocks(o_m), from_blocks(o_d)


def sample_mixers(q_nope, q_pe, ckv, kpe, q1, q2, dk, dv, lam, rel_bias, w_uk, w_uv, g_kn,
                  layer, c_ckv, c_kpe, c_dk, c_dv, page_table):
    T = q_nope.shape[1]
    past = page_table.shape[1] * c_ckv.shape[2]
    kpos = jnp.arange(past + T, dtype=jnp.int32)
    qpos = kpos[past:]

    def gather(cache, pt, new):
        rows = cache[layer, pt].reshape(past, cache.shape[-1])
        return jnp.concatenate([rows, new.astype(rows.dtype)], axis=0)[None]

    def one_seq(args):
        pt, qn, qp, a1, a2, ckv_n, kpe_n, dk_n, dv_n = args
        k_nope, v = mla_expand(gather(c_ckv, pt, ckv_n), w_uk, w_uv, g_kn)
        o_m = mla_attend(qn[None], qp[None], k_nope, gather(c_kpe, pt, kpe_n), v, qpos, kpos)
        dk_all = gather(c_dk, pt, dk_n)
        o_d = diff_attend(a1[None], a2[None], dk_all[..., :DIFF_HD], dk_all[..., DIFF_HD:],
                          gather(c_dv, pt, dv_n), lam, rel_bias, qpos, kpos)
        return o_m[0], o_d[0]

    return lax.map(one_seq, (page_table, q_nope, q_pe, q1, q2, ckv, kpe, dk, dv))


def merge_heads(o_mla, o_diff, g_sub, lam_init, w_o):
    B, T = o_mla.shape[:2]
    o_diff = rmsnorm(o_diff, g_sub) * (1.0 - lam_init)
    o = jnp.concatenate([o_mla.reshape(B, T, -1), o_diff.reshape(B, T, -1)], axis=-1)
    return o @ w_o


def setup_inputs(seed: int = 0) -> dict:
    key = jax.random.key(seed)
    ks = iter(jax.random.split(key, 48))
    f32 = jnp.float32
    n_pages = PAST_LEN // PAGE_SIZE
    n_pool = (5 * DEC_BATCH * n_pages + 3) // 4

    def nrm(shape, scale=1.0):
        return scale * jax.random.normal(next(ks), shape, f32)

    def gain(shape):
        return 1.0 + 0.02 * jax.random.normal(next(ks), shape, f32)

    x_prompt = nrm((BATCH, SEQ, D_MODEL))
    x_sample = nrm((DEC_BATCH, DEC_SEQ, D_MODEL))
    cache_mla_ckv = nrm((DEPTH, n_pool, PAGE_SIZE, MLA_KV_RANK))
    cache_mla_kpe = nrm((DEPTH, n_pool, PAGE_SIZE, MLA_ROPE))
    cache_diff_k = nrm((DEPTH, n_pool, PAGE_SIZE, 2 * DIFF_HD))
    cache_diff_v = nrm((DEPTH, n_pool, PAGE_SIZE, DIFF_VD))
    page_table = jax.random.permutation(next(ks), n_pool)[: DEC_BATCH * n_pages].reshape(
        DEC_BATCH, n_pages).astype(jnp.int32)
    return {
        'x_prompt': x_prompt,
        'x_sample': x_sample,
        'cache_mla_ckv': cache_mla_ckv,
        'cache_mla_kpe': cache_mla_kpe,
        'cache_diff_k': cache_diff_k,
        'cache_diff_v': cache_diff_v,
        'page_table': page_table,
        'g_ffn1': gain((DEPTH, D_MODEL)),
        'w1_gate': nrm((DEPTH, D_MODEL, D_FF), D_MODEL ** -0.5),
        'w1_up': nrm((DEPTH, D_MODEL, D_FF), D_MODEL ** -0.5),
        'w1_down': nrm((DEPTH, D_FF, D_MODEL), D_FF ** -0.5),
        'g_attn': gain((DEPTH, D_MODEL)),
        'w_in': nrm((DEPTH, D_MODEL, IN_COLS), D_MODEL ** -0.5),
        'g_cq': gain((DEPTH, MLA_Q_RANK)),
        'w_uq': nrm((DEPTH, MLA_Q_RANK, MLA_HEADS * (MLA_NOPE + MLA_ROPE)), MLA_Q_RANK ** -0.5),
        'g_qn': gain((DEPTH, MLA_NOPE)),
        'g_qr': gain((DEPTH, MLA_ROPE)),
        'g_ckv': gain((DEPTH, MLA_KV_RANK)),
        'g_kr': gain((DEPTH, MLA_ROPE)),
        'w_uk': nrm((DEPTH, MLA_KV_RANK, MLA_HEADS, MLA_NOPE), MLA_KV_RANK ** -0.5),
        'g_kn': gain((DEPTH, MLA_NOPE)),
        'w_uv': nrm((DEPTH, MLA_KV_RANK, MLA_HEADS, MLA_VDIM), MLA_KV_RANK ** -0.5),
        'g_dq': gain((DEPTH, 2, DIFF_HD)),
        'g_dk': gain((DEPTH, 2, DIFF_HD)),
        'lam_q1': nrm((DEPTH, DIFF_HD), 0.1),
        'lam_k1': nrm((DEPTH, DIFF_HD), 0.1),
        'lam_q2': nrm((DEPTH, DIFF_HD), 0.1),
        'lam_k2': nrm((DEPTH, DIFF_HD), 0.1),
        'g_sub': gain((DEPTH, DIFF_VD)),
        'rel_bias': nrm((N_BUCKETS, DIFF_HEADS), 0.5),
        'w_o': nrm((DEPTH, MIX_WIDTH, D_MODEL), MIX_WIDTH ** -0.5),
        'g_ffn2': gain((DEPTH, D_MODEL)),
        'w2_gate': nrm((DEPTH, D_MODEL, D_FF), D_MODEL ** -0.5),
        'w2_up': nrm((DEPTH, D_MODEL, D_FF), D_MODEL ** -0.5),
        'w2_down': nrm((DEPTH, D_FF, D_MODEL), D_FF ** -0.5),
    }


def reference(x_prompt, x_sample, cache_mla_ckv, cache_mla_kpe, cache_diff_k, cache_diff_v, page_table,
              g_ffn1, w1_gate, w1_up, w1_down, g_attn, w_in, g_cq, w_uq, g_qn, g_qr, g_ckv, g_kr,
              w_uk, g_kn, w_uv, g_dq, g_dk, lam_q1, lam_k1, lam_q2, lam_k2, g_sub, rel_bias, w_o,
              g_ffn2, w2_gate, w2_up, w2_down):
    f32 = jnp.float32
    pos_p = jnp.arange(x_prompt.shape[1], dtype=jnp.int32)
    pos_s = PAST_LEN + jnp.arange(x_sample.shape[1], dtype=jnp.int32)
    cos_p, sin_p = rope_angles(pos_p)
    cos_s, sin_s = rope_angles(pos_s)
    xp, xs = x_prompt, x_sample
    p_ckv, p_kpe, p_dk, p_dv = [], [], [], []
    s_ckv, s_kpe, s_dk, s_dv = [], [], [], []
    for l in range(DEPTH):
        lam_init = 0.8 - 0.6 * math.exp(-0.3 * l)
        lam = (jnp.exp(jnp.sum(lam_q1[l].astype(f32) * lam_k1[l].astype(f32)))
               - jnp.exp(jnp.sum(lam_q2[l].astype(f32) * lam_k2[l].astype(f32))) + lam_init)

        def step(x, cos, sin, mix):
            x = swiglu_half(x, g_ffn1[l], w1_gate[l], w1_up[l], w1_down[l])
            pr = project(rmsnorm(x, g_attn[l]), cos, sin, w_in[l], g_cq[l], w_uq[l], g_qn[l], g_qr[l],
                         g_ckv[l], g_kr[l], g_dq[l], g_dk[l])
            o_m, o_d = mix(*pr)
            x = x + merge_heads(o_m, o_d, g_sub[l], lam_init, w_o[l])
            x = swiglu_half(x, g_ffn2[l], w2_gate[l], w2_up[l], w2_down[l])
            return x, pr[2], pr[3], pr[6], pr[7]

        xp, ckv_p, kpe_p, dk_p, dv_p = step(
            xp, cos_p, sin_p,
            lambda *pr: prompt_mixers(*pr, lam, rel_bias, w_uk[l], w_uv[l], g_kn[l]))
        xs, ckv_s, kpe_s, dk_s, dv_s = step(
            xs, cos_s, sin_s,
            lambda *pr: sample_mixers(*pr, lam, rel_bias, w_uk[l], w_uv[l], g_kn[l], l,
                                      cache_mla_ckv, cache_mla_kpe, cache_diff_k, cache_diff_v,
                                      page_table))
        p_ckv.append(ckv_p); p_kpe.append(kpe_p); p_dk.append(dk_p); p_dv.append(dv_p)
        s_ckv.append(ckv_s); s_kpe.append(kpe_s); s_dk.append(dk_s); s_dv.append(dv_s)
    return (xp, xs,
            jnp.stack(p_ckv), jnp.stack(p_kpe), jnp.stack(p_dk), jnp.stack(p_dv),
            jnp.stack(s_ckv), jnp.stack(s_kpe), jnp.stack(s_dk), jnp.stack(s_dv))
```

```python
import functools
import math

import jax
import jax.numpy as jnp
import numpy as np
from jax import lax
from jax.experimental import pallas as pl
from jax.experimental.pallas import tpu as pltpu

F32 = jnp.float32
BF16 = jnp.bfloat16

EPS = 1e-6
HEADS = 8
NOPE = 128
ROPE = 64
Q_RANK = 512
KV_RANK = 256
DIFF_HD = 64
VD = 128
N_BUCKETS = 32
MAX_DISTANCE = 128
ROPE_THETA = 10000.0
MLA_SCALE = (NOPE + ROPE) ** -0.5
DIFF_SCALE = DIFF_HD ** -0.5
NEG = -1e30
LANE = 128
MXU_DIM = 256
VMEM_LIMIT = 56 * 1024 * 1024


def _cparams(sem):
    return pltpu.CompilerParams(dimension_semantics=sem, vmem_limit_bytes=VMEM_LIMIT)


def _dot(a, b):
    return jnp.dot(a, b, preferred_element_type=F32)


def _dot_nt(a, b):
    return lax.dot_general(a, b, (((1,), (1,)), ((), ())), preferred_element_type=F32)


def _rms_rows(x, g):
    return x * lax.rsqrt(jnp.mean(x * x, axis=-1, keepdims=True) + EPS) * g


def _group_sum(xs, group):
    width = xs.shape[-1]
    chunk = min(width, MXU_DIM)
    r = lax.broadcasted_iota(jnp.int32, (chunk, chunk), 0) // group
    c = lax.broadcasted_iota(jnp.int32, (chunk, chunk), 1) // group
    ones = (r == c).astype(BF16)
    outs = []
    for s in range(0, width, chunk):
        part = xs[:, s:s + chunk]
        hi = part.astype(BF16)
        lo = (part - hi.astype(F32)).astype(BF16)
        outs.append(_dot(hi, ones) + _dot(lo, ones))
    return outs[0] if len(outs) == 1 else jnp.concatenate(outs, axis=-1)


def _rotate_half_slot(x, cos, sin_signed):
    lane = lax.broadcasted_iota(jnp.int32, x.shape, 1)
    rot = jnp.where(lane < ROPE // 2, pltpu.roll(x, LANE - ROPE // 2, 1), pltpu.roll(x, ROPE // 2, 1))
    return x * cos + rot * sin_signed


def _ffn_kernel(x_ref, g_ref, wg_ref, wu_ref, wd_ref, o_ref, h_ref):
    @pl.when(pl.program_id(1) == 0)
    def _():
        x = x_ref[...]
        h_ref[...] = _rms_rows(x, g_ref[...]).astype(BF16)
        o_ref[...] = x

    h = h_ref[...]
    gate = _dot(h, wg_ref[...])
    up = _dot(h, wu_ref[...])
    act = (gate * jax.nn.sigmoid(gate) * up).astype(BF16)
    o_ref[...] += 0.5 * _dot(act, wd_ref[...])


def _ffn(x, g, wg, wu, wd, tm, tf):
    n, d = x.shape
    f = wg.shape[1]
    return pl.pallas_call(
        _ffn_kernel,
        grid=(n // tm, f // tf),
        in_specs=[
            pl.BlockSpec((tm, d), lambda i, j: (i, 0)),
            pl.BlockSpec((1, d), lambda i, j: (0, 0)),
            pl.BlockSpec((d, tf), lambda i, j: (0, j)),
            pl.BlockSpec((d, tf), lambda i, j: (0, j)),
            pl.BlockSpec((tf, d), lambda i, j: (j, 0)),
        ],
        out_specs=pl.BlockSpec((tm, d), lambda i, j: (i, 0)),
        out_shape=jax.ShapeDtypeStruct((n, d), F32),
        scratch_shapes=[pltpu.VMEM((tm, d), BF16)],
        compiler_params=_cparams(("parallel", "arbitrary")),
        name="ffn",
    )(x, g, wg, wu, wd)


_C_CQ = 0
_C_CKV = Q_RANK
_C_DQ = _C_CKV + KV_RANK
_C_DK = _C_DQ + HEADS * 2 * DIFF_HD
_C_DV = _C_DK + 2 * DIFF_HD
_C_KPE = _C_DV + VD
_IN_COLS = _C_KPE + LANE


def _proj_kernel(x_ref, gattn_ref, win_ref, gcq_ref, wuq_ref, gqn_ref, gqr_ref, gckv_ref, gkr_ref,
                 gdq_ref, gdk_ref, cos_ref, sin_ref,
                 ckv_ref, kpe_ref, dk_ref, dv_ref, qmla_ref, dq_ref, kpeb_ref, dkb_ref, dvb_ref):
    h = _rms_rows(x_ref[...], gattn_ref[...]).astype(BF16)
    y = _dot(h, win_ref[...])
    cos = cos_ref[...]
    sin = sin_ref[...]

    cq = _rms_rows(y[:, _C_CQ:_C_CKV], gcq_ref[...]).astype(BF16)
    q = _dot(cq, wuq_ref[...])
    nope_w = HEADS * NOPE
    qn = q[:, :nope_w]
    qn = qn * lax.rsqrt(_group_sum(qn * qn, NOPE) * (1.0 / NOPE) + EPS) * gqn_ref[...]
    qr = q[:, nope_w:]
    qr = qr * lax.rsqrt(_group_sum(qr * qr, LANE) * (1.0 / ROPE) + EPS) * gqr_ref[...]
    for hd in range(HEADS):
        lo = hd * (NOPE + LANE)
        qmla_ref[:, lo:lo + NOPE] = (qn[:, hd * NOPE:(hd + 1) * NOPE] * MLA_SCALE).astype(BF16)
        slot = _rotate_half_slot(qr[:, hd * LANE:(hd + 1) * LANE], cos, sin)
        qmla_ref[:, lo + NOPE:lo + NOPE + LANE] = (slot * MLA_SCALE).astype(BF16)

    ckv_ref[...] = _rms_rows(y[:, _C_CKV:_C_DQ], gckv_ref[...])

    kpe = y[:, _C_KPE:_IN_COLS]
    kpe = kpe * lax.rsqrt(jnp.sum(kpe * kpe, axis=-1, keepdims=True) * (1.0 / ROPE) + EPS) * gkr_ref[...]
    kpe = _rotate_half_slot(kpe, cos, sin)
    kpe_ref[...] = kpe
    kpeb_ref[...] = kpe.astype(BF16)

    dq = y[:, _C_DQ:_C_DK]
    dq = dq * lax.rsqrt(_group_sum(dq * dq, DIFF_HD) * (1.0 / DIFF_HD) + EPS) * gdq_ref[...]
    dq_ref[...] = (dq * DIFF_SCALE).astype(BF16)

    dk = y[:, _C_DK:_C_DV]
    dk = dk * lax.rsqrt(_group_sum(dk * dk, DIFF_HD) * (1.0 / DIFF_HD) + EPS) * gdk_ref[...]
    dk_ref[...] = dk
    dkb_ref[...] = dk.astype(BF16)

    dv = y[:, _C_DV:_C_KPE]
    dv_ref[...] = dv
    dvb_ref[...] = dv.astype(BF16)


def _proj(x, gattn, win, gcq, wuq, gqn, gqr, gckv, gkr, gdq, gdk, cos, sin, tm):
    n, d = x.shape
    row = lambda w: pl.BlockSpec((tm, w), lambda i: (i, 0))
    full = lambda a: pl.BlockSpec(a.shape, lambda i: (0, 0))
    qw = HEADS * (NOPE + LANE)
    outs = [(KV_RANK, F32), (LANE, F32), (2 * DIFF_HD, F32), (VD, F32), (qw, BF16),
            (HEADS * 2 * DIFF_HD, BF16), (LANE, BF16), (2 * DIFF_HD, BF16), (VD, BF16)]
    return pl.pallas_call(
        _proj_kernel,
        grid=(n // tm,),
        in_specs=[row(d), full(gattn), full(win), full(gcq), full(wuq), full(gqn), full(gqr), full(gckv),
                  full(gkr), full(gdq), full(gdk), row(LANE), row(LANE)],
        out_specs=[row(w) for w, _ in outs],
        out_shape=[jax.ShapeDtypeStruct((n, w), dt) for w, dt in outs],
        compiler_params=_cparams(("parallel",)),
        name="proj",
    )(x, gattn, win, gcq, wuq, gqn, gqr, gckv, gkr, gdq, gdk, cos, sin)


def _kvexp_kernel(ckv_ref, wuk_ref, wuv_ref, gkn_ref, k_ref, v_ref):
    c = ckv_ref[...].astype(BF16)
    k = _dot(c, wuk_ref[...])
    k = k * lax.rsqrt(_group_sum(k * k, NOPE) * (1.0 / NOPE) + EPS) * gkn_ref[...]
    k_ref[...] = k.astype(BF16)
    v_ref[...] = _dot(c, wuv_ref[...]).astype(BF16)


def _kvexp(ckv, wuk, wuv, gkn, n_rows, tm):
    w = HEADS * NOPE
    return pl.pallas_call(
        _kvexp_kernel,
        grid=(n_rows // tm,),
        in_specs=[pl.BlockSpec((tm, KV_RANK), lambda i: (i, 0)),
                  pl.BlockSpec(wuk.shape, lambda i: (0, 0)),
                  pl.BlockSpec(wuv.shape, lambda i: (0, 0)),
                  pl.BlockSpec(gkn.shape, lambda i: (0, 0))],
        out_specs=[pl.BlockSpec((tm, w), lambda i: (i, 0)), pl.BlockSpec((tm, w), lambda i: (i, 0))],
        out_shape=[jax.ShapeDtypeStruct((n_rows, w), BF16), jax.ShapeDtypeStruct((n_rows, w), BF16)],
        compiler_params=_cparams(("parallel",)),
        name="kvexp",
    )(ckv, wuk, wuv, gkn)


def _softmax_step(s, m_ref, l_ref):
    m_old = m_ref[...]
    m_new = jnp.maximum(m_old, jnp.max(s, axis=-1, keepdims=True))
    alpha = jnp.exp(m_old - m_new)
    p = jnp.exp(s - m_new)
    l_ref[...] = alpha * l_ref[...] + jnp.sum(p, axis=-1, keepdims=True)
    m_ref[...] = m_new
    return p, alpha


def _mla_prompt_kernel(q_ref, kn_ref, kpe_ref, v_ref, o_ref, m_ref, l_ref, acc_ref):
    qi = pl.program_id(2)
    ki = pl.program_id(3)

    @pl.when(ki == 0)
    def _():
        m_ref[...] = jnp.full(m_ref.shape, NEG, F32)
        l_ref[...] = jnp.zeros(l_ref.shape, F32)
        acc_ref[...] = jnp.zeros(acc_ref.shape, F32)

    def step(masked):
        k = jnp.concatenate([kn_ref[...], kpe_ref[...]], axis=-1)
        s = _dot_nt(q_ref[...], k)
        if masked:
            r = lax.broadcasted_iota(jnp.int32, s.shape, 0)
            c = lax.broadcasted_iota(jnp.int32, s.shape, 1)
            s = jnp.where(c <= r, s, NEG)
        p, alpha = _softmax_step(s, m_ref, l_ref)
        acc_ref[...] = alpha * acc_ref[...] + _dot(p.astype(BF16), v_ref[...])

    @pl.when(ki < qi)
    def _():
        step(False)

    @pl.when(ki == qi)
    def _():
        step(True)
        o_ref[...] = (acc_ref[...] / l_ref[...]).astype(o_ref.dtype)


def _mla_prompt(q, kn, kpe, v, batch, seq, tq):
    nq = seq // tq
    grid = (batch, HEADS, nq, nq)
    kmap = lambda b, h, qi, ki: (b * nq + jnp.minimum(ki, qi), h)
    return pl.pallas_call(
        _mla_prompt_kernel,
        grid=grid,
        in_specs=[pl.BlockSpec((tq, NOPE + LANE), lambda b, h, qi, ki: (b * nq + qi, h)),
                  pl.BlockSpec((tq, NOPE), kmap),
                  pl.BlockSpec((tq, LANE), lambda b, h, qi, ki: (b * nq + jnp.minimum(ki, qi), 0)),
                  pl.BlockSpec((tq, VD), kmap)],
        out_specs=pl.BlockSpec((tq, VD), lambda b, h, qi, ki: (b * nq + qi, h)),
        out_shape=jax.ShapeDtypeStruct((batch * seq, HEADS * VD), BF16),
        scratch_shapes=[pltpu.VMEM((tq, 1), F32), pltpu.VMEM((tq, 1), F32), pltpu.VMEM((tq, VD), F32)],
        compiler_params=_cparams(("parallel", "parallel", "parallel", "arbitrary")),
        name="mla_prompt",
    )(q, kn, kpe, v)


def _lambda_value(lam_ref, lam_init):
    lam = lam_ref[...]
    a = jnp.exp(jnp.sum(lam[0:1] * lam[1:2], axis=-1, keepdims=True))
    b = jnp.exp(jnp.sum(lam[2:3] * lam[3:4], axis=-1, keepdims=True))
    return a - b + lam_init


def _sub_norm(o, gsub, lam_init):
    return _rms_rows(o, gsub) * (1.0 - lam_init)


def _diff_prompt_kernel(far_ref, q_ref, k_ref, v_ref, bias_ref, lam_ref, gsub_ref, o_ref,
                        m1_ref, l1_ref, a1_ref, m2_ref, l2_ref, a2_ref, *, lam_init):
    h = pl.program_id(1)
    qi = pl.program_id(2)
    ki = pl.program_id(3)

    @pl.when(ki == 0)
    def _():
        for m_ref, l_ref, a_ref in ((m1_ref, l1_ref, a1_ref), (m2_ref, l2_ref, a2_ref)):
            m_ref[...] = jnp.full(m_ref.shape, NEG, F32)
            l_ref[...] = jnp.zeros(l_ref.shape, F32)
            a_ref[...] = jnp.zeros(a_ref.shape, F32)

    def step(bias):
        q = q_ref[...]
        lane = lax.broadcasted_iota(jnp.int32, q.shape, 1)
        zero = jnp.zeros_like(q)
        k = k_ref[...]
        v = v_ref[...]
        for qm, m_ref, l_ref, a_ref in ((jnp.where(lane < DIFF_HD, q, zero), m1_ref, l1_ref, a1_ref),
                                        (jnp.where(lane >= DIFF_HD, q, zero), m2_ref, l2_ref, a2_ref)):
            s = _dot_nt(qm, k) + bias
            p, alpha = _softmax_step(s, m_ref, l_ref)
            a_ref[...] = alpha * a_ref[...] + _dot(p.astype(BF16), v)

    @pl.when(ki < qi - 1)
    def _():
        step(far_ref[h])

    @pl.when(ki == qi - 1)
    def _():
        step(bias_ref[1])

    @pl.when(ki == qi)
    def _():
        step(bias_ref[0])
        lam = _lambda_value(lam_ref, lam_init)
        o = a1_ref[...] / l1_ref[...] - lam * (a2_ref[...] / l2_ref[...])
        o_ref[...] = _sub_norm(o, gsub_ref[...], lam_init).astype(o_ref.dtype)


def _diff_prompt(far, q, k, v, bias, lam, gsub, batch, seq, tq, lam_init):
    nq = seq // tq
    kmap = lambda b, h, qi, ki: (b * nq + jnp.minimum(ki, qi), 0)
    stat = lambda w: pltpu.VMEM((tq, w), F32)
    return pl.pallas_call(
        functools.partial(_diff_prompt_kernel, lam_init=lam_init),
        grid=(batch, HEADS, nq, nq),
        in_specs=[pl.BlockSpec(memory_space=pltpu.SMEM),
                  pl.BlockSpec((tq, 2 * DIFF_HD), lambda b, h, qi, ki: (b * nq + qi, h)),
                  pl.BlockSpec((tq, 2 * DIFF_HD), kmap),
                  pl.BlockSpec((tq, VD), kmap),
                  pl.BlockSpec((None, 2, tq, tq), lambda b, h, qi, ki: (h, 0, 0, 0)),
                  pl.BlockSpec(lam.shape, lambda b, h, qi, ki: (0, 0)),
                  pl.BlockSpec(gsub.shape, lambda b, h, qi, ki: (0, 0))],
        out_specs=pl.BlockSpec((tq, VD), lambda b, h, qi, ki: (b * nq + qi, h)),
        out_shape=jax.ShapeDtypeStruct((batch * seq, HEADS * VD), BF16),
        scratch_shapes=[stat(1), stat(1), stat(VD), stat(1), stat(1), stat(VD)],
        compiler_params=_cparams(("parallel", "parallel", "parallel", "arbitrary")),
        name="diff_prompt",
    )(far, q, k, v, bias, lam, gsub)


def _absorb_kernel(qn_ref, w_ref, o_ref):
    o_ref[...] = _dot(qn_ref[...], w_ref[...]).astype(o_ref.dtype)


def _absorb(qmla, wabs):
    n = qmla.shape[0]
    return pl.pallas_call(
        _absorb_kernel,
        grid=(HEADS,),
        in_specs=[pl.BlockSpec((n, NOPE), lambda h: (0, 2 * h)),
                  pl.BlockSpec((NOPE, KV_RANK), lambda h: (h, 0))],
        out_specs=pl.BlockSpec((n, KV_RANK), lambda h: (0, h)),
        out_shape=jax.ShapeDtypeStruct((n, HEADS * KV_RANK), F32),
        compiler_params=_cparams(("parallel",)),
        name="absorb",
    )(qmla, wabs)


def _uvexp_kernel(o_ref, w_ref, out_ref):
    out_ref[...] = _dot(o_ref[...], w_ref[...]).astype(out_ref.dtype)


def _uvexp(olat, wuv):
    n = olat.shape[0]
    return pl.pallas_call(
        _uvexp_kernel,
        grid=(HEADS,),
        in_specs=[pl.BlockSpec((n, KV_RANK), lambda h: (0, h)),
                  pl.BlockSpec((KV_RANK, VD), lambda h: (0, h))],
        out_specs=pl.BlockSpec((n, VD), lambda h: (0, h)),
        out_shape=jax.ShapeDtypeStruct((n, HEADS * VD), BF16),
        compiler_params=_cparams(("parallel",)),
        name="uvexp",
    )(olat, wuv)


def _sample_kernel(pt_ref, qw_ref, qm_ref, dq_ref, nckv_ref, nkpe_ref, ndk_ref, ndv_ref, wukt_ref,
                   blast_ref, bnew_ref, mnew_ref, bfar_ref, lam_ref, gsub_ref, *rest, pps, lam_init):
    del pt_ref
    ckv_pages = rest[0 * pps:1 * pps]
    kpe_pages = rest[1 * pps:2 * pps]
    dk_pages = rest[2 * pps:3 * pps]
    dv_pages = rest[3 * pps:4 * pps]
    olat_ref, odiff_ref = rest[4 * pps:4 * pps + 2]
    (qw_s, qpe_s, qd_s, ma_ref, la_ref, acca_ref, md_ref, ld_ref, accd_ref) = rest[4 * pps + 2:]
    nq = qw_ref.shape[0]
    rows = HEADS * nq
    j = pl.program_id(1)
    last = pl.num_programs(1) - 1

    @pl.when(j == 0)
    def _():
        dq = dq_ref[...]
        lane = lax.broadcasted_iota(jnp.int32, (nq, 2 * DIFF_HD), 1)
        for hd in range(HEADS):
            r0 = hd * nq
            qw_s[r0:r0 + nq, :] = qw_ref[:, hd * KV_RANK:(hd + 1) * KV_RANK]
            lo = hd * (NOPE + LANE) + NOPE
            qpe_s[r0:r0 + nq, :] = qm_ref[:, lo:lo + LANE]
            dqh = dq[:, hd * 2 * DIFF_HD:(hd + 1) * 2 * DIFF_HD]
            qd_s[r0:r0 + nq, :] = jnp.where(lane < DIFF_HD, dqh, 0.0)
            qd_s[rows + r0:rows + r0 + nq, :] = jnp.where(lane >= DIFF_HD, dqh, 0.0)
        ma_ref[...] = jnp.full(ma_ref.shape, NEG, F32)
        la_ref[...] = jnp.zeros(la_ref.shape, F32)
        acca_ref[...] = jnp.zeros(acca_ref.shape, F32)
        md_ref[...] = jnp.full(md_ref.shape, NEG, F32)
        ld_ref[...] = jnp.zeros(ld_ref.shape, F32)
        accd_ref[...] = jnp.zeros(accd_ref.shape, F32)

    def process(ckv, kpe, dk, dv, bias_d, mask_a):
        kt = _dot_nt(wukt_ref[...], ckv)
        sn = _dot_nt(qw_s[...].astype(BF16), ckv)
        sp = _dot_nt(qpe_s[:, :ROPE].astype(BF16), kpe)
        parts = []
        for hd in range(HEADS):
            kh = kt[hd * NOPE:(hd + 1) * NOPE, :]
            r = lax.rsqrt(jnp.sum(kh * kh, axis=0, keepdims=True) * (1.0 / NOPE) + EPS)
            parts.append(sn[hd * nq:(hd + 1) * nq, :] * r)
        s = jnp.concatenate(parts, axis=0) + sp
        if mask_a is not None:
            s = s + mask_a
        p, alpha = _softmax_step(s, ma_ref, la_ref)
        acca_ref[...] = alpha * acca_ref[...] + _dot(p.astype(BF16), ckv)
        sd = _dot_nt(qd_s[...].astype(BF16), dk)
        sd = sd + jnp.concatenate([bias_d, bias_d], axis=0)
        pd, alphad = _softmax_step(sd, md_ref, ld_ref)
        accd_ref[...] = alphad * accd_ref[...] + _dot(pd.astype(BF16), dv)

    def gather(pages):
        return jnp.concatenate([p[...].astype(BF16) for p in pages], axis=0)

    @pl.when(j < last)
    def _():
        process(gather(ckv_pages), gather(kpe_pages), gather(dk_pages), gather(dv_pages),
                bfar_ref[...], None)

    @pl.when(j == last)
    def _():
        process(gather(ckv_pages), gather(kpe_pages), gather(dk_pages), gather(dv_pages),
                blast_ref[...], None)

        def padded(ref):
            x = ref[...].astype(F32)
            return jnp.concatenate([x, jnp.zeros((LANE - nq, x.shape[1]), F32)], axis=0).astype(BF16)

        process(padded(nckv_ref), padded(nkpe_ref)[:, :ROPE], padded(ndk_ref), padded(ndv_ref),
                bnew_ref[...], mnew_ref[...])

        oa = acca_ref[...] / la_ref[...]
        acc = accd_ref[...]
        ld = ld_ref[...]
        lam = _lambda_value(lam_ref, lam_init)
        od = acc[:rows] / ld[:rows] - lam * (acc[rows:] / ld[rows:])
        od = _sub_norm(od, gsub_ref[...], lam_init)
        for hd in range(HEADS):
            olat_ref[:, hd * KV_RANK:(hd + 1) * KV_RANK] = oa[hd * nq:(hd + 1) * nq].astype(olat_ref.dtype)
            odiff_ref[:, hd * VD:(hd + 1) * VD] = od[hd * nq:(hd + 1) * nq].astype(odiff_ref.dtype)


def _sample_attention(layer, page_table, qw, qm, dq, nckv, nkpe, ndk, ndv, wukt, blast, bnew, mnew, bfar,
                      lam, gsub, c_ckv, c_kpe, c_dk, c_dv, pps, lam_init):
    db, nq, _ = qw.shape
    n_pages = page_table.shape[1]
    page = c_ckv.shape[2]
    nj = n_pages // pps
    rows = HEADS * nq

    seq3 = lambda a: pl.BlockSpec((None,) + a.shape[1:], lambda b, j, pt: (b, 0, 0))
    const = lambda a: pl.BlockSpec(a.shape, lambda b, j, pt: (0,) * a.ndim)

    def page_specs(cache):
        feat = cache.shape[-1]
        return [pl.BlockSpec((None, None, page, feat),
                             functools.partial(lambda b, j, pt, i: (layer, pt[b, j * pps + i], 0, 0), i=i))
                for i in range(pps)]

    in_specs = ([seq3(qw), seq3(qm), seq3(dq), seq3(nckv), seq3(nkpe), seq3(ndk), seq3(ndv), const(wukt),
                 const(blast), const(bnew), const(mnew), const(bfar), const(lam), const(gsub)]
                + page_specs(c_ckv) + page_specs(c_kpe) + page_specs(c_dk) + page_specs(c_dv))
    out_specs = [pl.BlockSpec((None, nq, HEADS * KV_RANK), lambda b, j, pt: (b, 0, 0)),
                 pl.BlockSpec((None, nq, HEADS * VD), lambda b, j, pt: (b, 0, 0))]
    scratch = [pltpu.VMEM((rows, KV_RANK), F32), pltpu.VMEM((rows, LANE), F32),
               pltpu.VMEM((2 * rows, 2 * DIFF_HD), F32),
               pltpu.VMEM((rows, 1), F32), pltpu.VMEM((rows, 1), F32), pltpu.VMEM((rows, KV_RANK), F32),
               pltpu.VMEM((2 * rows, 1), F32), pltpu.VMEM((2 * rows, 1), F32), pltpu.VMEM((2 * rows, VD), F32)]
    return pl.pallas_call(
        functools.partial(_sample_kernel, pps=pps, lam_init=lam_init),
        grid_spec=pltpu.PrefetchScalarGridSpec(
            num_scalar_prefetch=1, grid=(db, nj), in_specs=in_specs, out_specs=out_specs,
            scratch_shapes=scratch),
        out_shape=[jax.ShapeDtypeStruct((db, nq, HEADS * KV_RANK), BF16),
                   jax.ShapeDtypeStruct((db, nq, HEADS * VD), BF16)],
        compiler_params=_cparams(("parallel", "arbitrary")),
        name="sample_attn",
    )(page_table, qw, qm, dq, nckv, nkpe, ndk, ndv, wukt, blast, bnew, mnew, bfar, lam, gsub,
      *([c_ckv] * pps), *([c_kpe] * pps), *([c_dk] * pps), *([c_dv] * pps))


def _merge_kernel(x_ref, om_ref, od_ref, wa_ref, wb_ref, o_ref):
    o_ref[...] = x_ref[...] + _dot(om_ref[...], wa_ref[...]) + _dot(od_ref[...], wb_ref[...])


def _merge(x, om, od, wa, wb, tm):
    n, d = x.shape
    w = om.shape[1]
    return pl.pallas_call(
        _merge_kernel,
        grid=(n // tm,),
        in_specs=[pl.BlockSpec((tm, d), lambda i: (i, 0)),
                  pl.BlockSpec((tm, w), lambda i: (i, 0)),
                  pl.BlockSpec((tm, w), lambda i: (i, 0)),
                  pl.BlockSpec(wa.shape, lambda i: (0, 0)),
                  pl.BlockSpec(wb.shape, lambda i: (0, 0))],
        out_specs=pl.BlockSpec((tm, d), lambda i: (i, 0)),
        out_shape=jax.ShapeDtypeStruct((n, d), F32),
        compiler_params=_cparams(("parallel",)),
        name="merge",
    )(x, om, od, wa, wb)


def _t5_bucket(dist):
    n = jnp.maximum(dist, 0)
    max_exact = N_BUCKETS // 2
    nf = jnp.maximum(n, 1).astype(F32)
    large = max_exact + (jnp.log(nf / max_exact) / math.log(MAX_DISTANCE / max_exact)
                         * (N_BUCKETS - max_exact)).astype(jnp.int32)
    large = jnp.minimum(large, N_BUCKETS - 1)
    return jnp.where(n < max_exact, n, large)


def _bias_table(rel_bias, dist):
    b = jnp.moveaxis(rel_bias[_t5_bucket(dist)].astype(F32), -1, 0)
    return jnp.where(dist[None] >= 0, b, NEG)


def _rope_tables(pos):
    inv = ROPE_THETA ** (-jnp.arange(0, ROPE, 2, dtype=F32) / ROPE)
    ang = pos.astype(F32)[:, None] * inv[None, :]
    c, s = jnp.cos(ang), jnp.sin(ang)
    z = jnp.zeros((pos.shape[0], LANE - ROPE), F32)
    return jnp.concatenate([c, c, z], axis=-1), jnp.concatenate([-s, s, z], axis=-1)


def _pick_tile(n, prefs):
    for t in prefs:
        if n % t == 0:
            return t
    return n


def kernel(x_prompt, x_sample, cache_mla_ckv, cache_mla_kpe, cache_diff_k, cache_diff_v, page_table,
           g_ffn1, w1_gate, w1_up, w1_down, g_attn, w_in, g_cq, w_uq, g_qn, g_qr, g_ckv, g_kr,
           w_uk, g_kn, w_uv, g_dq, g_dk, lam_q1, lam_k1, lam_q2, lam_k2, g_sub, rel_bias, w_o,
           g_ffn2, w2_gate, w2_up, w2_down):
    batch, seq, d = x_prompt.shape
    db, nq, _ = x_sample.shape
    depth = g_ffn1.shape[0]
    n_pages = page_table.shape[1]
    page = cache_mla_ckv.shape[2]
    past = n_pages * page
    n_p = batch * seq
    n_s = db * nq
    n = n_p + n_s

    tm = _pick_tile(math.gcd(n_p, n_s), (512, 256, 128))
    tq = _pick_tile(seq, (512, 256, 128))
    pps = _pick_tile(n_pages, (8, 4, 2, 1))
    tile_keys = pps * page

    pos = jnp.concatenate([jnp.tile(jnp.arange(seq, dtype=jnp.int32), batch),
                           jnp.tile(past + jnp.arange(nq, dtype=jnp.int32), db)])
    cos_t, sin_t = _rope_tables(pos)

    ar = jnp.arange(tq, dtype=jnp.int32)
    d0 = ar[:, None] - ar[None, :]
    bias_prompt = jnp.stack([_bias_table(rel_bias, d0), _bias_table(rel_bias, d0 + tq)], axis=1)
    far = rel_bias[_t5_bucket(jnp.int32(2 * MAX_DISTANCE))].astype(F32)
    qa = jnp.arange(nq, dtype=jnp.int32)
    rows = HEADS * nq
    d_last = tile_keys + qa[:, None] - jnp.arange(tile_keys, dtype=jnp.int32)[None, :]
    bias_last = _bias_table(rel_bias, d_last).reshape(rows, tile_keys)
    i_new = jnp.arange(LANE, dtype=jnp.int32)
    d_new = jnp.where(i_new[None, :] < nq, qa[:, None] - i_new[None, :], -1)
    bias_new = _bias_table(rel_bias, d_new).reshape(rows, LANE)
    mask_new = jnp.tile(jnp.where(d_new >= 0, 0.0, NEG).astype(F32), (HEADS, 1))
    bias_far = jnp.repeat(far, nq)[:, None]

    f = w1_gate.shape[-1]
    tf = 512
    f_pad = -(-f // tf) * tf

    def ffn_weights(wg, wu, wd):
        padc = lambda w: jnp.pad(w.astype(BF16), ((0, 0), (0, f_pad - f)))
        return padc(wg), padc(wu), jnp.pad(wd.astype(BF16), ((0, f_pad - f), (0, 0)))

    row = lambda v: v.reshape(1, -1).astype(F32)
    pad_to = lambda v, w: jnp.pad(v, ((0, 0), (0, w - v.shape[1])))

    xs = jnp.concatenate([x_prompt.reshape(n_p, d), x_sample.reshape(n_s, d)], axis=0)
    outs = [[] for _ in range(8)]
    for l in range(depth):
        lam_init = 0.8 - 0.6 * math.exp(-0.3 * l)

        wi = w_in[l]
        o_cq, o_ckv, o_kpe, o_dq, o_dk, o_dv = 0, Q_RANK, Q_RANK + KV_RANK, Q_RANK + KV_RANK + ROPE, \
            Q_RANK + KV_RANK + ROPE + HEADS * 2 * DIFF_HD, Q_RANK + KV_RANK + ROPE + HEADS * 2 * DIFF_HD + 2 * DIFF_HD
        win = jnp.concatenate([wi[:, o_cq:o_kpe], wi[:, o_dq:], wi[:, o_kpe:o_dq],
                               jnp.zeros((d, LANE - ROPE), wi.dtype)], axis=1).astype(BF16)
        wq = w_uq[l].reshape(Q_RANK, HEADS, NOPE + ROPE)
        wq_rope = jnp.pad(wq[:, :, NOPE:], ((0, 0), (0, 0), (0, LANE - ROPE)))
        wuq = jnp.concatenate([wq[:, :, :NOPE].reshape(Q_RANK, HEADS * NOPE),
                               wq_rope.reshape(Q_RANK, HEADS * LANE)], axis=1).astype(BF16)
        gqn = jnp.tile(row(g_qn[l]), (1, HEADS))
        gqr = jnp.tile(pad_to(row(g_qr[l]), LANE), (1, HEADS))
        gkr = pad_to(row(g_kr[l]), LANE)
        gdq = jnp.tile(row(g_dq[l]), (1, HEADS))
        gdk = row(g_dk[l])
        gkn = jnp.tile(row(g_kn[l]), (1, HEADS))
        wuk2 = w_uk[l].reshape(KV_RANK, HEADS * NOPE)
        wuv2 = w_uv[l].reshape(KV_RANK, HEADS * VD).astype(BF16)
        wukt = wuk2.T
        wabs = (wukt * gkn.reshape(-1, 1)).astype(BF16)
        lam = jnp.stack([lam_q1[l], lam_k1[l], lam_q2[l], lam_k2[l]]).astype(F32)
        gsub = row(g_sub[l])
        wo = w_o[l].astype(BF16)

        xs = _ffn(xs, row(g_ffn1[l]), *ffn_weights(w1_gate[l], w1_up[l], w1_down[l]), tm, tf)
        ckv, kpe, dk, dv, qmla, dq, kpeb, dkb, dvb = _proj(
            xs, row(g_attn[l]), win, row(g_cq[l]), wuq, gqn, gqr, row(g_ckv[l]), gkr, gdq, gdk,
            cos_t, sin_t, tm)

        kn, v = _kvexp(ckv, wuk2.astype(BF16), wuv2, gkn, n_p, tm)
        om_p = _mla_prompt(qmla, kn, kpeb, v, batch, seq, tq)
        od_p = _diff_prompt(far, dq, dkb, dvb, bias_prompt, lam, gsub, batch, seq, tq, lam_init)

        qm_s = qmla[n_p:]
        qw = _absorb(qm_s, wabs)
        three = lambda a: a.reshape(db, nq, a.shape[-1])
        olat, od_s = _sample_attention(
            l, page_table, three(qw), three(qm_s.astype(F32)), three(dq[n_p:].astype(F32)),
            three(ckv[n_p:]), three(kpe[n_p:]), three(dk[n_p:]), three(dv[n_p:]),
            wukt.astype(BF16), bias_last, bias_new, mask_new, bias_far, lam, gsub,
            cache_mla_ckv, cache_mla_kpe, cache_diff_k, cache_diff_v, pps, lam_init)
        om_s = _uvexp(olat.reshape(n_s, HEADS * KV_RANK), wuv2)

        om = jnp.concatenate([om_p, om_s], axis=0)
        od = jnp.concatenate([od_p, od_s.reshape(n_s, HEADS * VD)], axis=0)
        xs = _merge(xs, om, od, wo[:HEADS * VD], wo[HEADS * VD:], tm)
        xs = _ffn(xs, row(g_ffn2[l]), *ffn_weights(w2_gate[l], w2_up[l], w2_down[l]), tm, tf)

        for lst, a, w in zip(outs[:4], (ckv, kpe, dk, dv), (KV_RANK, ROPE, 2 * DIFF_HD, VD)):
            lst.append(a[:n_p, :w].reshape(batch, seq, w))
        for lst, a, w in zip(outs[4:], (ckv, kpe, dk, dv), (KV_RANK, ROPE, 2 * DIFF_HD, VD)):
            lst.append(a[n_p:, :w].reshape(db, nq, w))

    return (xs[:n_p].reshape(batch, seq, d), xs[n_p:].reshape(db, nq, d)) + tuple(jnp.stack(o) for o in outs)
```

```python
import functools
import math

import jax
import jax.numpy as jnp
import numpy as np
from jax import lax
from jax.experimental import pallas as pl
from jax.experimental.pallas import tpu as pltpu

F32 = jnp.float32
BF16 = jnp.bfloat16

EPS = 1e-6
HEADS = 8
NOPE = 128
ROPE = 64
Q_RANK = 512
KV_RANK = 256
DIFF_HD = 64
VD = 128
N_BUCKETS = 32
MAX_DISTANCE = 128
ROPE_THETA = 10000.0
MLA_SCALE = (NOPE + ROPE) ** -0.5
DIFF_SCALE = DIFF_HD ** -0.5
NEG = -1e30
LANE = 128
MXU_DIM = 256
VMEM_LIMIT = 56 * 1024 * 1024


def _cparams(sem):
    return pltpu.CompilerParams(dimension_semantics=sem, vmem_limit_bytes=VMEM_LIMIT)


def _dot(a, b):
    return jnp.dot(a, b, preferred_element_type=F32)


def _dot_nt(a, b):
    return lax.dot_general(a, b, (((1,), (1,)), ((), ())), preferred_element_type=F32)


def _rms_rows(x, g):
    return x * lax.rsqrt(jnp.mean(x * x, axis=-1, keepdims=True) + EPS) * g


def _group_sum(xs, group):
    width = xs.shape[-1]
    chunk = min(width, MXU_DIM)
    r = lax.broadcasted_iota(jnp.int32, (chunk, chunk), 0) // group
    c = lax.broadcasted_iota(jnp.int32, (chunk, chunk), 1) // group
    ones = (r == c).astype(BF16)
    outs = []
    for s in range(0, width, chunk):
        part = xs[:, s:s + chunk]
        hi = part.astype(BF16)
        lo = (part - hi.astype(F32)).astype(BF16)
        outs.append(_dot(hi, ones) + _dot(lo, ones))
    return outs[0] if len(outs) == 1 else jnp.concatenate(outs, axis=-1)


def _rotate_half_slot(x, cos, sin_signed):
    lane = lax.broadcasted_iota(jnp.int32, x.shape, 1)
    rot = jnp.where(lane < ROPE // 2, pltpu.roll(x, LANE - ROPE // 2, 1), pltpu.roll(x, ROPE // 2, 1))
    return x * cos + rot * sin_signed


def _ffn_kernel(x_ref, g_ref, wg_ref, wu_ref, wd_ref, o_ref, h_ref):
    @pl.when(pl.program_id(1) == 0)
    def _():
        x = x_ref[...]
        h_ref[...] = _rms_rows(x, g_ref[...]).astype(BF16)
        o_ref[...] = x

    h = h_ref[...]
    gate = _dot(h, wg_ref[...])
    up = _dot(h, wu_ref[...])
    act = (gate * jax.nn.sigmoid(gate) * up).astype(BF16)
    o_ref[...] += 0.5 * _dot(act, wd_ref[...])


def _ffn(x, g, wg, wu, wd, tm, tf):
    n, d = x.shape
    f = wg.shape[1]
    return pl.pallas_call(
        _ffn_kernel,
        grid=(n // tm, f // tf),
        in_specs=[
            pl.BlockSpec((tm, d), lambda i, j: (i, 0)),
            pl.BlockSpec((1, d), lambda i, j: (0, 0)),
            pl.BlockSpec((d, tf), lambda i, j: (0, j)),
            pl.BlockSpec((d, tf), lambda i, j: (0, j)),
            pl.BlockSpec((tf, d), lambda i, j: (j, 0)),
        ],
        out_specs=pl.BlockSpec((tm, d), lambda i, j: (i, 0)),
        out_shape=jax.ShapeDtypeStruct((n, d), F32),
        scratch_shapes=[pltpu.VMEM((tm, d), BF16)],
        compiler_params=_cparams(("parallel", "arbitrary")),
        name="ffn",
    )(x, g, wg, wu, wd)


_C_CQ = 0
_C_CKV = Q_RANK
_C_DQ = _C_CKV + KV_RANK
_C_DK = _C_DQ + HEADS * 2 * DIFF_HD
_C_DV = _C_DK + 2 * DIFF_HD
_C_KPE = _C_DV + VD
_IN_COLS = _C_KPE + LANE


def _proj_kernel(x_ref, gattn_ref, win_ref, gcq_ref, wuq_ref, gqn_ref, gqr_ref, gckv_ref, gkr_ref,
                 gdq_ref, gdk_ref, cos_ref, sin_ref,
                 ckv_ref, kpe_ref, dk_ref, dv_ref, qmla_ref, dq_ref, kpeb_ref, dkb_ref, dvb_ref):
    h = _rms_rows(x_ref[...], gattn_ref[...]).astype(BF16)
    y = _dot(h, win_ref[...])
    cos = cos_ref[...]
    sin = sin_ref[...]

    cq = _rms_rows(y[:, _C_CQ:_C_CKV], gcq_ref[...]).astype(BF16)
    q = _dot(cq, wuq_ref[...])
    nope_w = HEADS * NOPE
    qn = q[:, :nope_w]
    qn = qn * lax.rsqrt(_group_sum(qn * qn, NOPE) * (1.0 / NOPE) + EPS) * gqn_ref[...]
    qr = q[:, nope_w:]
    qr = qr * lax.rsqrt(_group_sum(qr * qr, LANE) * (1.0 / ROPE) + EPS) * gqr_ref[...]
    for hd in range(HEADS):
        lo = hd * (NOPE + LANE)
        qmla_ref[:, lo:lo + NOPE] = (qn[:, hd * NOPE:(hd + 1) * NOPE] * MLA_SCALE).astype(BF16)
        slot = _rotate_half_slot(qr[:, hd * LANE:(hd + 1) * LANE], cos, sin)
        qmla_ref[:, lo + NOPE:lo + NOPE + LANE] = (slot * MLA_SCALE).astype(BF16)

    ckv_ref[...] = _rms_rows(y[:, _C_CKV:_C_DQ], gckv_ref[...])

    kpe = y[:, _C_KPE:_IN_COLS]
    kpe = kpe * lax.rsqrt(jnp.sum(kpe * kpe, axis=-1, keepdims=True) * (1.0 / ROPE) + EPS) * gkr_ref[...]
    kpe = _rotate_half_slot(kpe, cos, sin)
    kpe_ref[...] = kpe
    kpeb_ref[...] = kpe.astype(BF16)

    dq = y[:, _C_DQ:_C_DK]
    dq = dq * lax.rsqrt(_group_sum(dq * dq, DIFF_HD) * (1.0 / DIFF_HD) + EPS) * gdq_ref[...]
    dq_ref[...] = (dq * DIFF_SCALE).astype(BF16)

    dk = y[:, _C_DK:_C_DV]
    dk = dk * lax.rsqrt(_group_sum(dk * dk, DIFF_HD) * (1.0 / DIFF_HD) + EPS) * gdk_ref[...]
    dk_ref[...] = dk
    dkb_ref[...] = dk.astype(BF16)

    dv = y[:, _C_DV:_C_KPE]
    dv_ref[...] = dv
    dvb_ref[...] = dv.astype(BF16)


def _proj(x, gattn, win, gcq, wuq, gqn, gqr, gckv, gkr, gdq, gdk, cos, sin, tm):
    n, d = x.shape
    row = lambda w: pl.BlockSpec((tm, w), lambda i: (i, 0))
    full = lambda a: pl.BlockSpec(a.shape, lambda i: (0, 0))
    qw = HEADS * (NOPE + LANE)
    outs = [(KV_RANK, F32), (LANE, F32), (2 * DIFF_HD, F32), (VD, F32), (qw, BF16),
            (HEADS * 2 * DIFF_HD, BF16), (LANE, BF16), (2 * DIFF_HD, BF16), (VD, BF16)]
    return pl.pallas_call(
        _proj_kernel,
        grid=(n // tm,),
        in_specs=[row(d), full(gattn), full(win), full(gcq), full(wuq), full(gqn), full(gqr), full(gckv),
                  full(gkr), full(gdq), full(gdk), row(LANE), row(LANE)],
        out_specs=[row(w) for w, _ in outs],
        out_shape=[jax.ShapeDtypeStruct((n, w), dt) for w, dt in outs],
        compiler_params=_cparams(("parallel",)),
        name="proj",
    )(x, gattn, win, gcq, wuq, gqn, gqr, gckv, gkr, gdq, gdk, cos, sin)


def _kvexp_kernel(ckv_ref, wuk_ref, wuv_ref, gkn_ref, k_ref, v_ref):
    c = ckv_ref[...].astype(BF16)
    k = _dot(c, wuk_ref[...])
    k = k * lax.rsqrt(_group_sum(k * k, NOPE) * (1.0 / NOPE) + EPS) * gkn_ref[...]
    k_ref[...] = k.astype(BF16)
    v_ref[...] = _dot(c, wuv_ref[...]).astype(BF16)


def _kvexp(ckv, wuk, wuv, gkn, n_rows, tm):
    w = HEADS * NOPE
    return pl.pallas_call(
        _kvexp_kernel,
        grid=(n_rows // tm,),
        in_specs=[pl.BlockSpec((tm, KV_RANK), lambda i: (i, 0)),
                  pl.BlockSpec(wuk.shape, lambda i: (0, 0)),
                  pl.BlockSpec(wuv.shape, lambda i: (0, 0)),
                  pl.BlockSpec(gkn.shape, lambda i: (0, 0))],
        out_specs=[pl.BlockSpec((tm, w), lambda i: (i, 0)), pl.BlockSpec((tm, w), lambda i: (i, 0))],
        out_shape=[jax.ShapeDtypeStruct((n_rows, w), BF16), jax.ShapeDtypeStruct((n_rows, w), BF16)],
        compiler_params=_cparams(("parallel",)),
        name="kvexp",
    )(ckv, wuk, wuv, gkn)


def _softmax_update(s, m_old, l_old):
    m_new = jnp.maximum(m_old, jnp.max(s, axis=-1, keepdims=True))
    alpha = jnp.exp(m_old - m_new)
    p = jnp.exp(s - m_new)
    l_new = alpha * l_old + jnp.sum(p, axis=-1, keepdims=True)
    return p, alpha, m_new, l_new


def _init_softmax_state(m_ref, l_ref, acc_ref):
    m_ref[...] = jnp.full(m_ref.shape, NEG, F32)
    l_ref[...] = jnp.zeros(l_ref.shape, F32)
    acc_ref[...] = jnp.zeros(acc_ref.shape, F32)


def _mla_prompt_kernel(q_ref, kn_ref, kpe_ref, v_ref, o_ref, m_ref, l_ref, acc_ref):
    qi = pl.program_id(1)
    ki = pl.program_id(2)

    @pl.when(ki == 0)
    def _():
        _init_softmax_state(m_ref, l_ref, acc_ref)

    def step(masked):
        kpe = kpe_ref[...]
        for hd in range(HEADS):
            q = q_ref[:, hd * (NOPE + LANE):(hd + 1) * (NOPE + LANE)]
            k = jnp.concatenate([kn_ref[:, hd * NOPE:(hd + 1) * NOPE], kpe], axis=-1)
            s = _dot_nt(q, k)
            if masked:
                r = lax.broadcasted_iota(jnp.int32, s.shape, 0)
                c = lax.broadcasted_iota(jnp.int32, s.shape, 1)
                s = jnp.where(c <= r, s, NEG)
            p, alpha, m_new, l_new = _softmax_update(s, m_ref[hd], l_ref[hd])
            m_ref[hd] = m_new
            l_ref[hd] = l_new
            acc_ref[hd] = alpha * acc_ref[hd] + _dot(p.astype(BF16), v_ref[:, hd * VD:(hd + 1) * VD])

    @pl.when(ki < qi)
    def _():
        step(False)

    @pl.when(ki == qi)
    def _():
        step(True)
        for hd in range(HEADS):
            o_ref[:, hd * VD:(hd + 1) * VD] = (acc_ref[hd] / l_ref[hd]).astype(o_ref.dtype)


def _mla_prompt(q, kn, kpe, v, batch, seq, tq):
    nq = seq // tq
    qmap = lambda b, qi, ki: (b * nq + qi, 0)
    kmap = lambda b, qi, ki: (b * nq + jnp.minimum(ki, qi), 0)
    return pl.pallas_call(
        _mla_prompt_kernel,
        grid=(batch, nq, nq),
        in_specs=[pl.BlockSpec((tq, HEADS * (NOPE + LANE)), qmap),
                  pl.BlockSpec((tq, HEADS * NOPE), kmap),
                  pl.BlockSpec((tq, LANE), kmap),
                  pl.BlockSpec((tq, HEADS * VD), kmap)],
        out_specs=pl.BlockSpec((tq, HEADS * VD), qmap),
        out_shape=jax.ShapeDtypeStruct((batch * seq, HEADS * VD), BF16),
        scratch_shapes=[pltpu.VMEM((HEADS, tq, 1), F32), pltpu.VMEM((HEADS, tq, 1), F32),
                        pltpu.VMEM((HEADS, tq, VD), F32)],
        compiler_params=_cparams(("parallel", "parallel", "arbitrary")),
        name="mla_prompt",
    )(q, kn, kpe, v)


def _lambda_value(lam_ref, lam_init):
    lam = lam_ref[...]
    a = jnp.exp(jnp.sum(lam[0:1] * lam[1:2], axis=-1, keepdims=True))
    b = jnp.exp(jnp.sum(lam[2:3] * lam[3:4], axis=-1, keepdims=True))
    return a - b + lam_init


def _sub_norm(o, gsub, lam_init):
    return _rms_rows(o, gsub) * (1.0 - lam_init)


def _diff_prompt_kernel(far_ref, q_ref, k_ref, v_ref, bias_ref, lam_ref, gsub_ref, o_ref,
                        m_ref, l_ref, acc_ref, *, lam_init):
    qi = pl.program_id(1)
    ki = pl.program_id(2)

    @pl.when(ki == 0)
    def _():
        _init_softmax_state(m_ref, l_ref, acc_ref)

    def step(bias_of_head):
        k = k_ref[...]
        v = v_ref[...]
        lane = lax.broadcasted_iota(jnp.int32, (q_ref.shape[0], 2 * DIFF_HD), 1)
        for hd in range(HEADS):
            q = q_ref[:, hd * 2 * DIFF_HD:(hd + 1) * 2 * DIFF_HD]
            bias = bias_of_head(hd)
            for mp, qm in enumerate((jnp.where(lane < DIFF_HD, q, jnp.zeros_like(q)),
                                     jnp.where(lane >= DIFF_HD, q, jnp.zeros_like(q)))):
                i = 2 * hd + mp
                s = _dot_nt(qm, k) + bias
                p, alpha, m_new, l_new = _softmax_update(s, m_ref[i], l_ref[i])
                m_ref[i] = m_new
                l_ref[i] = l_new
                acc_ref[i] = alpha * acc_ref[i] + _dot(p.astype(BF16), v)

    @pl.when(ki < qi - 1)
    def _():
        step(lambda hd: far_ref[hd])

    @pl.when(ki == qi - 1)
    def _():
        step(lambda hd: bias_ref[hd])

    @pl.when(ki == qi)
    def _():
        step(lambda hd: bias_ref[hd])
        lam = _lambda_value(lam_ref, lam_init)
        for hd in range(HEADS):
            o = acc_ref[2 * hd] / l_ref[2 * hd] - lam * (acc_ref[2 * hd + 1] / l_ref[2 * hd + 1])
            o_ref[:, hd * VD:(hd + 1) * VD] = _sub_norm(o, gsub_ref[...], lam_init).astype(o_ref.dtype)


def _diff_prompt(far, q, k, v, bias, lam, gsub, batch, seq, tq, lam_init):
    nq = seq // tq
    qmap = lambda b, qi, ki: (b * nq + qi, 0)
    kmap = lambda b, qi, ki: (b * nq + jnp.minimum(ki, qi), 0)
    return pl.pallas_call(
        functools.partial(_diff_prompt_kernel, lam_init=lam_init),
        grid=(batch, nq, nq),
        in_specs=[pl.BlockSpec(memory_space=pltpu.SMEM),
                  pl.BlockSpec((tq, HEADS * 2 * DIFF_HD), qmap),
                  pl.BlockSpec((tq, 2 * DIFF_HD), kmap),
                  pl.BlockSpec((tq, VD), kmap),
                  pl.BlockSpec((None, HEADS, tq, tq),
                               lambda b, qi, ki: (jnp.where(ki >= qi, 0, 1), 0, 0, 0)),
                  pl.BlockSpec(lam.shape, lambda b, qi, ki: (0, 0)),
                  pl.BlockSpec(gsub.shape, lambda b, qi, ki: (0, 0))],
        out_specs=pl.BlockSpec((tq, HEADS * VD), qmap),
        out_shape=jax.ShapeDtypeStruct((batch * seq, HEADS * VD), BF16),
        scratch_shapes=[pltpu.VMEM((2 * HEADS, tq, 1), F32), pltpu.VMEM((2 * HEADS, tq, 1), F32),
                        pltpu.VMEM((2 * HEADS, tq, VD), F32)],
        compiler_params=_cparams(("parallel", "parallel", "arbitrary")),
        name="diff_prompt",
    )(far, q, k, v, bias, lam, gsub)


def _absorb_kernel(qn_ref, w_ref, o_ref):
    o_ref[...] = _dot(qn_ref[...], w_ref[...]).astype(o_ref.dtype)


def _absorb(qmla, wabs):
    n = qmla.shape[0]
    return pl.pallas_call(
        _absorb_kernel,
        grid=(HEADS,),
        in_specs=[pl.BlockSpec((n, NOPE), lambda h: (0, 2 * h)),
                  pl.BlockSpec((NOPE, KV_RANK), lambda h: (h, 0))],
        out_specs=pl.BlockSpec((n, KV_RANK), lambda h: (0, h)),
        out_shape=jax.ShapeDtypeStruct((n, HEADS * KV_RANK), F32),
        compiler_params=_cparams(("parallel",)),
        name="absorb",
    )(qmla, wabs)


def _uvexp_kernel(o_ref, w_ref, out_ref):
    out_ref[...] = _dot(o_ref[...], w_ref[...]).astype(out_ref.dtype)


def _uvexp(olat, wuv):
    n = olat.shape[0]
    return pl.pallas_call(
        _uvexp_kernel,
        grid=(HEADS,),
        in_specs=[pl.BlockSpec((n, KV_RANK), lambda h: (0, h)),
                  pl.BlockSpec((KV_RANK, VD), lambda h: (0, h))],
        out_specs=pl.BlockSpec((n, VD), lambda h: (0, h)),
        out_shape=jax.ShapeDtypeStruct((n, HEADS * VD), BF16),
        compiler_params=_cparams(("parallel",)),
        name="uvexp",
    )(olat, wuv)


def _sample_kernel(pt_ref, qw_ref, qm_ref, dq_ref, nckv_ref, nkpe_ref, ndk_ref, ndv_ref, wukt_ref,
                   btile_ref, bnew_ref, mnew_ref, bfar_ref, lam_ref, gsub_ref, *rest, pps, cpp, lam_init):
    del pt_ref
    ckv_pages, kpe_pages, dk_pages, dv_pages = (rest[i * pps:(i + 1) * pps] for i in range(4))
    olat_ref, odiff_ref = rest[4 * pps:4 * pps + 2]
    qw_s, qpe_s, qd_s = rest[4 * pps + 2:4 * pps + 5]
    state_refs = rest[4 * pps + 5:]
    nq = qw_ref.shape[0]
    rows = HEADS * nq
    j = pl.program_id(1)
    last = pl.num_programs(1) - 1

    @pl.when(j == 0)
    def _():
        dq = dq_ref[...]
        lane = lax.broadcasted_iota(jnp.int32, (nq, 2 * DIFF_HD), 1)
        for hd in range(HEADS):
            r0 = hd * nq
            qw_s[r0:r0 + nq, :] = qw_ref[:, hd * KV_RANK:(hd + 1) * KV_RANK]
            lo = hd * (NOPE + LANE) + NOPE
            qpe_s[r0:r0 + nq, :] = qm_ref[:, lo:lo + LANE]
            dqh = dq[:, hd * 2 * DIFF_HD:(hd + 1) * 2 * DIFF_HD]
            qd_s[r0:r0 + nq, :] = jnp.where(lane < DIFF_HD, dqh, 0.0)
            qd_s[rows + r0:rows + r0 + nq, :] = jnp.where(lane >= DIFF_HD, dqh, 0.0)
        _init_softmax_state(*state_refs[:3])
        _init_softmax_state(*state_refs[3:])

    qw = qw_s[...].astype(BF16)
    qpe = qpe_s[:, :ROPE].astype(BF16)
    qd = qd_s[...].astype(BF16)

    def scores(ckv, sp, dk, bias_d, mask_a):
        kt = _dot_nt(wukt_ref[...], ckv)
        sn = _dot_nt(qw, ckv)
        parts = []
        for hd in range(HEADS):
            kh = kt[hd * NOPE:(hd + 1) * NOPE, :]
            r = lax.rsqrt(jnp.sum(kh * kh, axis=0, keepdims=True) * (1.0 / NOPE) + EPS)
            parts.append(sn[hd * nq:(hd + 1) * nq, :] * r)
        s = jnp.concatenate(parts, axis=0) + sp
        if mask_a is not None:
            s = s + mask_a
        sd = _dot_nt(qd, dk) + jnp.concatenate([bias_d, bias_d], axis=0)
        return s, sd

    def accumulate(state, s, sd, ckv, dv):
        ma, la, acca, md, ld, accd = state
        p, alpha, ma, la = _softmax_update(s, ma, la)
        acca = alpha * acca + _dot(p.astype(BF16), ckv)
        pd, alphad, md, ld = _softmax_update(sd, md, ld)
        accd = alphad * accd + _dot(pd.astype(BF16), dv)
        return ma, la, acca, md, ld, accd

    def cat(pages, axis):
        return jnp.concatenate([p[...].astype(BF16) for p in pages], axis=axis)

    n_chunks = pps // cpp
    s_parts, sd_parts, ckv_parts = [], [], []
    for c in range(n_chunks):
        sl = slice(c * cpp, (c + 1) * cpp)
        ckv = cat(ckv_pages[sl], 0)
        sp = _dot(qpe, cat(kpe_pages[sl], 1))
        bias = btile_ref[...] if c == n_chunks - 1 else bfar_ref[...]
        s, sd = scores(ckv, sp, cat(dk_pages[sl], 0), bias, None)
        s_parts.append(s)
        sd_parts.append(sd)
        ckv_parts.append(ckv)
    state = accumulate(tuple(r[...] for r in state_refs), jnp.concatenate(s_parts, axis=1),
                       jnp.concatenate(sd_parts, axis=1), jnp.concatenate(ckv_parts, axis=0), cat(dv_pages, 0))
    for r, val in zip(state_refs, state):
        r[...] = val

    @pl.when(j == last)
    def _():
        def padded(ref):
            x = ref[...].astype(F32)
            return jnp.concatenate([x, jnp.zeros((LANE - nq, x.shape[1]), F32)], axis=0).astype(BF16)

        sp = _dot_nt(qpe, padded(nkpe_ref)[:, :ROPE])
        nckv = padded(nckv_ref)
        s, sd = scores(nckv, sp, padded(ndk_ref), bnew_ref[...], mnew_ref[...])
        ma, la, acca, md, ld, accd = accumulate(tuple(r[...] for r in state_refs), s, sd, nckv, padded(ndv_ref))
        oa = acca / la
        lam = _lambda_value(lam_ref, lam_init)
        od = accd[:rows] / ld[:rows] - lam * (accd[rows:] / ld[rows:])
        od = _sub_norm(od, gsub_ref[...], lam_init)
        for hd in range(HEADS):
            olat_ref[:, hd * KV_RANK:(hd + 1) * KV_RANK] = oa[hd * nq:(hd + 1) * nq].astype(olat_ref.dtype)
            odiff_ref[:, hd * VD:(hd + 1) * VD] = od[hd * nq:(hd + 1) * nq].astype(odiff_ref.dtype)


def _sample_attention(layer, page_table, qw, qm, dq, nckv, nkpe, ndk, ndv, wukt, btile, bnew, mnew, bfar,
                      lam, gsub, c_ckv, c_kpe_t, c_dk, c_dv, pps, cpp, lam_init):
    db, nq, _ = qw.shape
    n_pages = page_table.shape[1]
    nj = n_pages // pps
    rows = HEADS * nq

    seq3 = lambda a: pl.BlockSpec((None,) + a.shape[1:], lambda b, j, pt: (b, 0, 0))
    const = lambda a: pl.BlockSpec(a.shape, lambda b, j, pt: (0,) * a.ndim)

    def page_specs(cache):
        return [pl.BlockSpec((None, None) + cache.shape[2:],
                             functools.partial(lambda b, j, pt, i: (layer, pt[b, j * pps + i], 0, 0), i=i))
                for i in range(pps)]

    btile_spec = pl.BlockSpec((None,) + btile.shape[1:], lambda b, j, pt: (jnp.where(j == nj - 1, 1, 0), 0, 0))
    in_specs = ([seq3(qw), seq3(qm), seq3(dq), seq3(nckv), seq3(nkpe), seq3(ndk), seq3(ndv), const(wukt),
                 btile_spec, const(bnew), const(mnew), const(bfar), const(lam), const(gsub)]
                + page_specs(c_ckv) + page_specs(c_kpe_t) + page_specs(c_dk) + page_specs(c_dv))
    out_specs = [pl.BlockSpec((None, nq, HEADS * KV_RANK), lambda b, j, pt: (b, 0, 0)),
                 pl.BlockSpec((None, nq, HEADS * VD), lambda b, j, pt: (b, 0, 0))]
    scratch = [pltpu.VMEM((rows, KV_RANK), F32), pltpu.VMEM((rows, LANE), F32),
               pltpu.VMEM((2 * rows, 2 * DIFF_HD), F32),
               pltpu.VMEM((rows, 1), F32), pltpu.VMEM((rows, 1), F32), pltpu.VMEM((rows, KV_RANK), F32),
               pltpu.VMEM((2 * rows, 1), F32), pltpu.VMEM((2 * rows, 1), F32), pltpu.VMEM((2 * rows, VD), F32)]
    return pl.pallas_call(
        functools.partial(_sample_kernel, pps=pps, cpp=cpp, lam_init=lam_init),
        grid_spec=pltpu.PrefetchScalarGridSpec(
            num_scalar_prefetch=1, grid=(db, nj), in_specs=in_specs, out_specs=out_specs,
            scratch_shapes=scratch),
        out_shape=[jax.ShapeDtypeStruct((db, nq, HEADS * KV_RANK), BF16),
                   jax.ShapeDtypeStruct((db, nq, HEADS * VD), BF16)],
        compiler_params=_cparams(("parallel", "arbitrary")),
        name="sample_attn",
    )(page_table, qw, qm, dq, nckv, nkpe, ndk, ndv, wukt, btile, bnew, mnew, bfar, lam, gsub,
      *([c_ckv] * pps), *([c_kpe_t] * pps), *([c_dk] * pps), *([c_dv] * pps))


def _merge_kernel(x_ref, om_ref, od_ref, wa_ref, wb_ref, o_ref):
    o_ref[...] = x_ref[...] + _dot(om_ref[...], wa_ref[...]) + _dot(od_ref[...], wb_ref[...])


def _merge(x, om, od, wa, wb, tm):
    n, d = x.shape
    w = om.shape[1]
    return pl.pallas_call(
        _merge_kernel,
        grid=(n // tm,),
        in_specs=[pl.BlockSpec((tm, d), lambda i: (i, 0)),
                  pl.BlockSpec((tm, w), lambda i: (i, 0)),
                  pl.BlockSpec((tm, w), lambda i: (i, 0)),
                  pl.BlockSpec(wa.shape, lambda i: (0, 0)),
                  pl.BlockSpec(wb.shape, lambda i: (0, 0))],
        out_specs=pl.BlockSpec((tm, d), lambda i: (i, 0)),
        out_shape=jax.ShapeDtypeStruct((n, d), F32),
        compiler_params=_cparams(("parallel",)),
        name="merge",
    )(x, om, od, wa, wb)


def _t5_bucket(dist):
    n = jnp.maximum(dist, 0)
    max_exact = N_BUCKETS // 2
    nf = jnp.maximum(n, 1).astype(F32)
    large = max_exact + (jnp.log(nf / max_exact) / math.log(MAX_DISTANCE / max_exact)
                         * (N_BUCKETS - max_exact)).astype(jnp.int32)
    large = jnp.minimum(large, N_BUCKETS - 1)
    return jnp.where(n < max_exact, n, large)


def _bias_table(rel_bias, dist):
    bucket = _t5_bucket(dist)[None]
    out = jnp.zeros((rel_bias.shape[1],) + dist.shape, F32)
    for k in range(N_BUCKETS):
        val = rel_bias[k].astype(F32).reshape((-1,) + (1,) * dist.ndim)
        out = jnp.where(bucket == k, val, out)
    return jnp.where(dist[None] >= 0, out, NEG)


def _rope_tables(pos):
    inv = ROPE_THETA ** (-jnp.arange(0, ROPE, 2, dtype=F32) / ROPE)
    ang = pos.astype(F32)[:, None] * inv[None, :]
    c, s = jnp.cos(ang), jnp.sin(ang)
    z = jnp.zeros((pos.shape[0], LANE - ROPE), F32)
    return jnp.concatenate([c, c, z], axis=-1), jnp.concatenate([-s, s, z], axis=-1)


def _pick_tile(n, prefs):
    for t in prefs:
        if n % t == 0:
            return t
    return n


def kernel(x_prompt, x_sample, cache_mla_ckv, cache_mla_kpe, cache_diff_k, cache_diff_v, page_table,
           g_ffn1, w1_gate, w1_up, w1_down, g_attn, w_in, g_cq, w_uq, g_qn, g_qr, g_ckv, g_kr,
           w_uk, g_kn, w_uv, g_dq, g_dk, lam_q1, lam_k1, lam_q2, lam_k2, g_sub, rel_bias, w_o,
           g_ffn2, w2_gate, w2_up, w2_down):
    batch, seq, d = x_prompt.shape
    db, nq, _ = x_sample.shape
    depth = g_ffn1.shape[0]
    n_pages = page_table.shape[1]
    page = cache_mla_ckv.shape[2]
    past = n_pages * page
    n_p = batch * seq
    n_s = db * nq
    n = n_p + n_s

    tm = _pick_tile(math.gcd(n_p, n_s), (512, 256, 128))
    tq = _pick_tile(seq, (512, 256, 128))
    pps = _pick_tile(n_pages, (16, 8, 4, 2, 1))
    cpp = pps
    chunk_keys = cpp * page

    pos = jnp.concatenate([jnp.tile(jnp.arange(seq, dtype=jnp.int32), batch),
                           jnp.tile(past + jnp.arange(nq, dtype=jnp.int32), db)])
    cos_t, sin_t = _rope_tables(pos)

    ar = jnp.arange(tq, dtype=jnp.int32)
    d0 = ar[:, None] - ar[None, :]
    bias_prompt = jnp.stack([_bias_table(rel_bias, d0), _bias_table(rel_bias, d0 + tq)])
    far = rel_bias[_t5_bucket(jnp.int32(2 * MAX_DISTANCE))].astype(F32)
    qa = jnp.arange(nq, dtype=jnp.int32)
    rows = HEADS * nq
    d_last = chunk_keys + qa[:, None] - jnp.arange(chunk_keys, dtype=jnp.int32)[None, :]
    bias_last = _bias_table(rel_bias, d_last).reshape(rows, chunk_keys)
    i_new = jnp.arange(LANE, dtype=jnp.int32)
    d_new = jnp.where(i_new[None, :] < nq, qa[:, None] - i_new[None, :], -1)
    bias_new = _bias_table(rel_bias, d_new).reshape(rows, LANE)
    mask_new = jnp.tile(jnp.where(d_new >= 0, 0.0, NEG).astype(F32), (HEADS, 1))
    bias_far = jnp.repeat(far, nq)[:, None]
    bias_tile = jnp.stack([jnp.broadcast_to(bias_far, (rows, chunk_keys)), bias_last])
    cache_kpe_t = jnp.swapaxes(cache_mla_kpe, 2, 3)

    f = w1_gate.shape[-1]
    tf = 512
    f_pad = -(-f // tf) * tf

    def ffn_weights(wg, wu, wd):
        padc = lambda w: jnp.pad(w.astype(BF16), ((0, 0), (0, f_pad - f)))
        return padc(wg), padc(wu), jnp.pad(wd.astype(BF16), ((0, f_pad - f), (0, 0)))

    row = lambda v: v.reshape(1, -1).astype(F32)
    pad_to = lambda v, w: jnp.pad(v, ((0, 0), (0, w - v.shape[1])))

    xs = jnp.concatenate([x_prompt.reshape(n_p, d), x_sample.reshape(n_s, d)], axis=0)
    outs = [[] for _ in range(8)]
    for l in range(depth):
        lam_init = 0.8 - 0.6 * math.exp(-0.3 * l)

        wi = w_in[l]
        o_cq, o_ckv, o_kpe, o_dq, o_dk, o_dv = 0, Q_RANK, Q_RANK + KV_RANK, Q_RANK + KV_RANK + ROPE, \
            Q_RANK + KV_RANK + ROPE + HEADS * 2 * DIFF_HD, Q_RANK + KV_RANK + ROPE + HEADS * 2 * DIFF_HD + 2 * DIFF_HD
        win = jnp.concatenate([wi[:, o_cq:o_kpe], wi[:, o_dq:], wi[:, o_kpe:o_dq],
                               jnp.zeros((d, LANE - ROPE), wi.dtype)], axis=1).astype(BF16)
        wq = w_uq[l].reshape(Q_RANK, HEADS, NOPE + ROPE)
        wq_rope = jnp.pad(wq[:, :, NOPE:], ((0, 0), (0, 0), (0, LANE - ROPE)))
        wuq = jnp.concatenate([wq[:, :, :NOPE].reshape(Q_RANK, HEADS * NOPE),
                               wq_rope.reshape(Q_RANK, HEADS * LANE)], axis=1).astype(BF16)
        gqn = jnp.tile(row(g_qn[l]), (1, HEADS))
        gqr = jnp.tile(pad_to(row(g_qr[l]), LANE), (1, HEADS))
        gkr = pad_to(row(g_kr[l]), LANE)
        gdq = jnp.tile(row(g_dq[l]), (1, HEADS))
        gdk = row(g_dk[l])
        gkn = jnp.tile(row(g_kn[l]), (1, HEADS))
        wuk2 = w_uk[l].reshape(KV_RANK, HEADS * NOPE)
        wuv2 = w_uv[l].reshape(KV_RANK, HEADS * VD).astype(BF16)
        wukt = wuk2.T
        wabs = (wukt * gkn.reshape(-1, 1)).astype(BF16)
        lam = jnp.stack([lam_q1[l], lam_k1[l], lam_q2[l], lam_k2[l]]).astype(F32)
        gsub = row(g_sub[l])
        wo = w_o[l].astype(BF16)

        xs = _ffn(xs, row(g_ffn1[l]), *ffn_weights(w1_gate[l], w1_up[l], w1_down[l]), tm, tf)
        ckv, kpe, dk, dv, qmla, dq, kpeb, dkb, dvb = _proj(
            xs, row(g_attn[l]), win, row(g_cq[l]), wuq, gqn, gqr, row(g_ckv[l]), gkr, gdq, gdk,
            cos_t, sin_t, tm)

        kn, v = _kvexp(ckv, wuk2.astype(BF16), wuv2, gkn, n_p, tm)
        om_p = _mla_prompt(qmla, kn, kpeb, v, batch, seq, tq)
        od_p = _diff_prompt(far, dq, dkb, dvb, bias_prompt, lam, gsub, batch, seq, tq, lam_init)

        qm_s = qmla[n_p:]
        qw = _absorb(qm_s, wabs)
        three = lambda a: a.reshape(db, nq, a.shape[-1])
        olat, od_s = _sample_attention(
            l, page_table, three(qw), three(qm_s.astype(F32)), three(dq[n_p:].astype(F32)),
            three(ckv[n_p:]), three(kpe[n_p:]), three(dk[n_p:]), three(dv[n_p:]),
            wukt.astype(BF16), bias_tile, bias_new, mask_new, bias_far, lam, gsub,
            cache_mla_ckv, cache_kpe_t, cache_diff_k, cache_diff_v, pps, cpp, lam_init)
        om_s = _uvexp(olat.reshape(n_s, HEADS * KV_RANK), wuv2)

        om = jnp.concatenate([om_p, om_s], axis=0)
        od = jnp.concatenate([od_p, od_s.reshape(n_s, HEADS * VD)], axis=0)
        xs = _merge(xs, om, od, wo[:HEADS * VD], wo[HEADS * VD:], tm)
        xs = _ffn(xs, row(g_ffn2[l]), *ffn_weights(w2_gate[l], w2_up[l], w2_down[l]), tm, tf)

        for lst, a, w in zip(outs[:4], (ckv, kpe, dk, dv), (KV_RANK, ROPE, 2 * DIFF_HD, VD)):
            lst.append(a[:n_p, :w].reshape(batch, seq, w))
        for lst, a, w in zip(outs[4:], (ckv, kpe, dk, dv), (KV_RANK, ROPE, 2 * DIFF_HD, VD)):
            lst.append(a[n_p:, :w].reshape(db, nq, w))

    return (xs[:n_p].reshape(batch, seq, d), xs[n_p:].reshape(db, nq, d)) + tuple(jnp.stack(o) for o in outs)
```

```python
import functools
import math

import jax
import jax.numpy as jnp
import numpy as np
from jax import lax
from jax.experimental import pallas as pl
from jax.experimental.pallas import tpu as pltpu

F32 = jnp.float32
BF16 = jnp.bfloat16

EPS = 1e-6
HEADS = 8
NOPE = 128
ROPE = 64
Q_RANK = 512
KV_RANK = 256
DIFF_HD = 64
VD = 128
N_BUCKETS = 32
MAX_DISTANCE = 128
ROPE_THETA = 10000.0
MLA_SCALE = (NOPE + ROPE) ** -0.5
DIFF_SCALE = DIFF_HD ** -0.5
NEG = -1e30
LANE = 128
MXU_DIM = 256
VMEM_LIMIT = 56 * 1024 * 1024


def _cparams(sem):
    return pltpu.CompilerParams(dimension_semantics=sem, vmem_limit_bytes=VMEM_LIMIT)


def _dot(a, b):
    return jnp.dot(a, b, preferred_element_type=F32)


def _dot_nt(a, b):
    return lax.dot_general(a, b, (((1,), (1,)), ((), ())), preferred_element_type=F32)


def _rms_rows(x, g):
    return x * lax.rsqrt(jnp.mean(x * x, axis=-1, keepdims=True) + EPS) * g


def _group_sum(xs, group):
    width = xs.shape[-1]
    chunk = min(width, MXU_DIM)
    r = lax.broadcasted_iota(jnp.int32, (chunk, chunk), 0) // group
    c = lax.broadcasted_iota(jnp.int32, (chunk, chunk), 1) // group
    ones = (r == c).astype(BF16)
    outs = []
    for s in range(0, width, chunk):
        part = xs[:, s:s + chunk]
        hi = part.astype(BF16)
        lo = (part - hi.astype(F32)).astype(BF16)
        outs.append(_dot(hi, ones) + _dot(lo, ones))
    return outs[0] if len(outs) == 1 else jnp.concatenate(outs, axis=-1)


def _rotate_half_slot(x, cos, sin_signed):
    lane = lax.broadcasted_iota(jnp.int32, x.shape, 1)
    rot = jnp.where(lane < ROPE // 2, pltpu.roll(x, LANE - ROPE // 2, 1), pltpu.roll(x, ROPE // 2, 1))
    return x * cos + rot * sin_signed


def _ffn_kernel(x_ref, g_ref, wg_ref, wu_ref, wd_ref, o_ref, h_ref):
    @pl.when(pl.program_id(1) == 0)
    def _():
        x = x_ref[...]
        h_ref[...] = _rms_rows(x, g_ref[...]).astype(BF16)
        o_ref[...] = x

    h = h_ref[...]
    gate = _dot(h, wg_ref[...])
    up = _dot(h, wu_ref[...])
    act = (gate * jax.nn.sigmoid(gate) * up).astype(BF16)
    o_ref[...] += 0.5 * _dot(act, wd_ref[...])


def _ffn(x, g, wg, wu, wd, tm, tf):
    n, d = x.shape
    f = wg.shape[1]
    return pl.pallas_call(
        _ffn_kernel,
        grid=(n // tm, f // tf),
        in_specs=[
            pl.BlockSpec((tm, d), lambda i, j: (i, 0)),
            pl.BlockSpec((1, d), lambda i, j: (0, 0)),
            pl.BlockSpec((d, tf), lambda i, j: (0, j)),
            pl.BlockSpec((d, tf), lambda i, j: (0, j)),
            pl.BlockSpec((tf, d), lambda i, j: (j, 0)),
        ],
        out_specs=pl.BlockSpec((tm, d), lambda i, j: (i, 0)),
        out_shape=jax.ShapeDtypeStruct((n, d), F32),
        scratch_shapes=[pltpu.VMEM((tm, d), BF16)],
        compiler_params=_cparams(("parallel", "arbitrary")),
        name="ffn",
    )(x, g, wg, wu, wd)


_C_CQ = 0
_C_CKV = Q_RANK
_C_DQ = _C_CKV + KV_RANK
_C_DK = _C_DQ + HEADS * 2 * DIFF_HD
_C_DV = _C_DK + 2 * DIFF_HD
_C_KPE = _C_DV + VD
_IN_COLS = _C_KPE + LANE


def _proj_kernel(x_ref, gattn_ref, win_ref, gcq_ref, wuq_ref, gqn_ref, gqr_ref, gckv_ref, gkr_ref,
                 gdq_ref, gdk_ref, cos_ref, sin_ref,
                 ckv_ref, kpe_ref, dk_ref, dv_ref, qmla_ref, dq_ref, kpeb_ref, dkb_ref, dvb_ref):
    h = _rms_rows(x_ref[...], gattn_ref[...]).astype(BF16)
    y = _dot(h, win_ref[...])
    cos = cos_ref[...]
    sin = sin_ref[...]

    cq = _rms_rows(y[:, _C_CQ:_C_CKV], gcq_ref[...]).astype(BF16)
    q = _dot(cq, wuq_ref[...])
    nope_w = HEADS * NOPE
    qn = q[:, :nope_w]
    qn = qn * lax.rsqrt(_group_sum(qn * qn, NOPE) * (1.0 / NOPE) + EPS) * gqn_ref[...]
    qr = q[:, nope_w:]
    qr = qr * lax.rsqrt(_group_sum(qr * qr, LANE) * (1.0 / ROPE) + EPS) * gqr_ref[...]
    for hd in range(HEADS):
        lo = hd * (NOPE + LANE)
        qmla_ref[:, lo:lo + NOPE] = (qn[:, hd * NOPE:(hd + 1) * NOPE] * MLA_SCALE).astype(BF16)
        slot = _rotate_half_slot(qr[:, hd * LANE:(hd + 1) * LANE], cos, sin)
        qmla_ref[:, lo + NOPE:lo + NOPE + LANE] = (slot * MLA_SCALE).astype(BF16)

    ckv_ref[...] = _rms_rows(y[:, _C_CKV:_C_DQ], gckv_ref[...])

    kpe = y[:, _C_KPE:_IN_COLS]
    kpe = kpe * lax.rsqrt(jnp.sum(kpe * kpe, axis=-1, keepdims=True) * (1.0 / ROPE) + EPS) * gkr_ref[...]
    kpe = _rotate_half_slot(kpe, cos, sin)
    kpe_ref[...] = kpe
    kpeb_ref[...] = kpe.astype(BF16)

    dq = y[:, _C_DQ:_C_DK]
    dq = dq * lax.rsqrt(_group_sum(dq * dq, DIFF_HD) * (1.0 / DIFF_HD) + EPS) * gdq_ref[...]
    dq_ref[...] = (dq * DIFF_SCALE).astype(BF16)

    dk = y[:, _C_DK:_C_DV]
    dk = dk * lax.rsqrt(_group_sum(dk * dk, DIFF_HD) * (1.0 / DIFF_HD) + EPS) * gdk_ref[...]
    dk_ref[...] = dk
    dkb_ref[...] = dk.astype(BF16)

    dv = y[:, _C_DV:_C_KPE]
    dv_ref[...] = dv
    dvb_ref[...] = dv.astype(BF16)


def _proj(x, gattn, win, gcq, wuq, gqn, gqr, gckv, gkr, gdq, gdk, cos, sin, tm):
    n, d = x.shape
    row = lambda w: pl.BlockSpec((tm, w), lambda i: (i, 0))
    full = lambda a: pl.BlockSpec(a.shape, lambda i: (0, 0))
    qw = HEADS * (NOPE + LANE)
    outs = [(KV_RANK, F32), (LANE, F32), (2 * DIFF_HD, F32), (VD, F32), (qw, BF16),
            (HEADS * 2 * DIFF_HD, BF16), (LANE, BF16), (2 * DIFF_HD, BF16), (VD, BF16)]
    return pl.pallas_call(
        _proj_kernel,
        grid=(n // tm,),
        in_specs=[row(d), full(gattn), full(win), full(gcq), full(wuq), full(gqn), full(gqr), full(gckv),
                  full(gkr), full(gdq), full(gdk), row(LANE), row(LANE)],
        out_specs=[row(w) for w, _ in outs],
        out_shape=[jax.ShapeDtypeStruct((n, w), dt) for w, dt in outs],
        compiler_params=_cparams(("parallel",)),
        name="proj",
    )(x, gattn, win, gcq, wuq, gqn, gqr, gckv, gkr, gdq, gdk, cos, sin)


def _kvexp_kernel(ckv_ref, wuk_ref, wuv_ref, gkn_ref, k_ref, v_ref):
    c = ckv_ref[...].astype(BF16)
    k = _dot(c, wuk_ref[...])
    k = k * lax.rsqrt(_group_sum(k * k, NOPE) * (1.0 / NOPE) + EPS) * gkn_ref[...]
    k_ref[...] = k.astype(BF16)
    v_ref[...] = _dot(c, wuv_ref[...]).astype(BF16)


def _kvexp(ckv, wuk, wuv, gkn, n_rows, tm):
    w = HEADS * NOPE
    return pl.pallas_call(
        _kvexp_kernel,
        grid=(n_rows // tm,),
        in_specs=[pl.BlockSpec((tm, KV_RANK), lambda i: (i, 0)),
                  pl.BlockSpec(wuk.shape, lambda i: (0, 0)),
                  pl.BlockSpec(wuv.shape, lambda i: (0, 0)),
                  pl.BlockSpec(gkn.shape, lambda i: (0, 0))],
        out_specs=[pl.BlockSpec((tm, w), lambda i: (i, 0)), pl.BlockSpec((tm, w), lambda i: (i, 0))],
        out_shape=[jax.ShapeDtypeStruct((n_rows, w), BF16), jax.ShapeDtypeStruct((n_rows, w), BF16)],
        compiler_params=_cparams(("parallel",)),
        name="kvexp",
    )(ckv, wuk, wuv, gkn)


def _softmax_update(s, m_old, l_old):
    m_new = jnp.maximum(m_old, jnp.max(s, axis=-1, keepdims=True))
    alpha = jnp.exp(m_old - m_new)
    p = jnp.exp(s - m_new)
    l_new = alpha * l_old + jnp.sum(p, axis=-1, keepdims=True)
    return p, alpha, m_new, l_new


def _init_softmax_state(m_ref, l_ref, acc_ref):
    m_ref[...] = jnp.full(m_ref.shape, NEG, F32)
    l_ref[...] = jnp.zeros(l_ref.shape, F32)
    acc_ref[...] = jnp.zeros(acc_ref.shape, F32)


def _mla_prompt_kernel(q_ref, kn_ref, kpe_ref, v_ref, o_ref, m_ref, l_ref, acc_ref):
    qi = pl.program_id(1)
    ki = pl.program_id(2)

    @pl.when(ki == 0)
    def _():
        _init_softmax_state(m_ref, l_ref, acc_ref)

    def step(masked):
        kpe = kpe_ref[...]
        for hd in range(HEADS):
            q = q_ref[:, hd * (NOPE + LANE):(hd + 1) * (NOPE + LANE)]
            k = jnp.concatenate([kn_ref[:, hd * NOPE:(hd + 1) * NOPE], kpe], axis=-1)
            s = _dot_nt(q, k)
            if masked:
                r = lax.broadcasted_iota(jnp.int32, s.shape, 0)
                c = lax.broadcasted_iota(jnp.int32, s.shape, 1)
                s = jnp.where(c <= r, s, NEG)
            p, alpha, m_new, l_new = _softmax_update(s, m_ref[hd], l_ref[hd])
            m_ref[hd] = m_new
            l_ref[hd] = l_new
            acc_ref[hd] = alpha * acc_ref[hd] + _dot(p.astype(BF16), v_ref[:, hd * VD:(hd + 1) * VD])

    @pl.when(ki < qi)
    def _():
        step(False)

    @pl.when(ki == qi)
    def _():
        step(True)
        for hd in range(HEADS):
            o_ref[:, hd * VD:(hd + 1) * VD] = (acc_ref[hd] / l_ref[hd]).astype(o_ref.dtype)


def _mla_prompt(q, kn, kpe, v, batch, seq, tq):
    nq = seq // tq
    qmap = lambda b, qi, ki: (b * nq + qi, 0)
    kmap = lambda b, qi, ki: (b * nq + jnp.minimum(ki, qi), 0)
    return pl.pallas_call(
        _mla_prompt_kernel,
        grid=(batch, nq, nq),
        in_specs=[pl.BlockSpec((tq, HEADS * (NOPE + LANE)), qmap),
                  pl.BlockSpec((tq, HEADS * NOPE), kmap),
                  pl.BlockSpec((tq, LANE), kmap),
                  pl.BlockSpec((tq, HEADS * VD), kmap)],
        out_specs=pl.BlockSpec((tq, HEADS * VD), qmap),
        out_shape=jax.ShapeDtypeStruct((batch * seq, HEADS * VD), BF16),
        scratch_shapes=[pltpu.VMEM((HEADS, tq, 1), F32), pltpu.VMEM((HEADS, tq, 1), F32),
                        pltpu.VMEM((HEADS, tq, VD), F32)],
        compiler_params=_cparams(("parallel", "parallel", "arbitrary")),
        name="mla_prompt",
    )(q, kn, kpe, v)


def _lambda_value(lam_ref, lam_init):
    lam = lam_ref[...]
    a = jnp.exp(jnp.sum(lam[0:1] * lam[1:2], axis=-1, keepdims=True))
    b = jnp.exp(jnp.sum(lam[2:3] * lam[3:4], axis=-1, keepdims=True))
    return a - b + lam_init


def _sub_norm(o, gsub, lam_init):
    return _rms_rows(o, gsub) * (1.0 - lam_init)


def _diff_prompt_kernel(far_ref, q_ref, k_ref, v_ref, bias_ref, lam_ref, gsub_ref, o_ref,
                        m_ref, l_ref, acc_ref, *, lam_init):
    h = pl.program_id(1)
    qi = pl.program_id(2)
    ki = pl.program_id(3)

    @pl.when(ki == 0)
    def _():
        _init_softmax_state(m_ref, l_ref, acc_ref)

    def step(bias):
        q = q_ref[...]
        k = k_ref[...]
        v = v_ref[...]
        lane = lax.broadcasted_iota(jnp.int32, q.shape, 1)
        for i, qm in enumerate((jnp.where(lane < DIFF_HD, q, jnp.zeros_like(q)),
                                jnp.where(lane >= DIFF_HD, q, jnp.zeros_like(q)))):
            s = _dot_nt(qm, k) + bias
            p, alpha, m_new, l_new = _softmax_update(s, m_ref[i], l_ref[i])
            m_ref[i] = m_new
            l_ref[i] = l_new
            acc_ref[i] = alpha * acc_ref[i] + _dot(p.astype(BF16), v)

    @pl.when(ki < qi - 1)
    def _():
        step(far_ref[h])

    @pl.when(ki == qi - 1)
    def _():
        step(bias_ref[1])

    @pl.when(ki == qi)
    def _():
        step(bias_ref[0])
        lam = _lambda_value(lam_ref, lam_init)
        o = acc_ref[0] / l_ref[0] - lam * (acc_ref[1] / l_ref[1])
        o_ref[...] = _sub_norm(o, gsub_ref[...], lam_init).astype(o_ref.dtype)


def _diff_prompt(far, q, k, v, bias, lam, gsub, batch, seq, tq, lam_init):
    nq = seq // tq
    kmap = lambda b, h, qi, ki: (b * nq + jnp.minimum(ki, qi), 0)
    return pl.pallas_call(
        functools.partial(_diff_prompt_kernel, lam_init=lam_init),
        grid=(batch, HEADS, nq, nq),
        in_specs=[pl.BlockSpec(memory_space=pltpu.SMEM),
                  pl.BlockSpec((tq, 2 * DIFF_HD), lambda b, h, qi, ki: (b * nq + qi, h)),
                  pl.BlockSpec((tq, 2 * DIFF_HD), kmap),
                  pl.BlockSpec((tq, VD), kmap),
                  pl.BlockSpec((None, 2, tq, tq), lambda b, h, qi, ki: (h, 0, 0, 0)),
                  pl.BlockSpec(lam.shape, lambda b, h, qi, ki: (0, 0)),
                  pl.BlockSpec(gsub.shape, lambda b, h, qi, ki: (0, 0))],
        out_specs=pl.BlockSpec((tq, VD), lambda b, h, qi, ki: (b * nq + qi, h)),
        out_shape=jax.ShapeDtypeStruct((batch * seq, HEADS * VD), BF16),
        scratch_shapes=[pltpu.VMEM((2, tq, 1), F32), pltpu.VMEM((2, tq, 1), F32), pltpu.VMEM((2, tq, VD), F32)],
        compiler_params=_cparams(("parallel", "parallel", "parallel", "arbitrary")),
        name="diff_prompt",
    )(far, q, k, v, bias, lam, gsub)


def _absorb_kernel(qn_ref, w_ref, o_ref):
    o_ref[...] = _dot(qn_ref[...], w_ref[...]).astype(o_ref.dtype)


def _absorb(qmla, wabs):
    n = qmla.shape[0]
    return pl.pallas_call(
        _absorb_kernel,
        grid=(HEADS,),
        in_specs=[pl.BlockSpec((n, NOPE), lambda h: (0, 2 * h)),
                  pl.BlockSpec((NOPE, KV_RANK), lambda h: (h, 0))],
        out_specs=pl.BlockSpec((n, KV_RANK), lambda h: (0, h)),
        out_shape=jax.ShapeDtypeStruct((n, HEADS * KV_RANK), F32),
        compiler_params=_cparams(("parallel",)),
        name="absorb",
    )(qmla, wabs)


def _uvexp_kernel(o_ref, w_ref, out_ref):
    out_ref[...] = _dot(o_ref[...], w_ref[...]).astype(out_ref.dtype)


def _uvexp(olat, wuv):
    n = olat.shape[0]
    return pl.pallas_call(
        _uvexp_kernel,
        grid=(HEADS,),
        in_specs=[pl.BlockSpec((n, KV_RANK), lambda h: (0, h)),
                  pl.BlockSpec((KV_RANK, VD), lambda h: (0, h))],
        out_specs=pl.BlockSpec((n, VD), lambda h: (0, h)),
        out_shape=jax.ShapeDtypeStruct((n, HEADS * VD), BF16),
        compiler_params=_cparams(("parallel",)),
        name="uvexp",
    )(olat, wuv)


def _sample_kernel(pt_ref, qw_ref, qm_ref, dq_ref, nckv_ref, nkpe_ref, ndk_ref, ndv_ref, wukt_ref,
                   btile_ref, bnew_ref, mnew_ref, bfar_ref, lam_ref, gsub_ref, *rest, pps, cpp, lam_init):
    del pt_ref
    ckv_pages, kpe_pages, dk_pages, dv_pages = (rest[i * pps:(i + 1) * pps] for i in range(4))
    olat_ref, odiff_ref = rest[4 * pps:4 * pps + 2]
    qw_s, qpe_s, qd_s = rest[4 * pps + 2:4 * pps + 5]
    state_refs = rest[4 * pps + 5:]
    nq = qw_ref.shape[0]
    rows = HEADS * nq
    j = pl.program_id(1)
    last = pl.num_programs(1) - 1

    @pl.when(j == 0)
    def _():
        dq = dq_ref[...]
        lane = lax.broadcasted_iota(jnp.int32, (nq, 2 * DIFF_HD), 1)
        for hd in range(HEADS):
            r0 = hd * nq
            qw_s[r0:r0 + nq, :] = qw_ref[:, hd * KV_RANK:(hd + 1) * KV_RANK]
            lo = hd * (NOPE + LANE) + NOPE
            qpe_s[r0:r0 + nq, :] = qm_ref[:, lo:lo + LANE]
            dqh = dq[:, hd * 2 * DIFF_HD:(hd + 1) * 2 * DIFF_HD]
            qd_s[r0:r0 + nq, :] = jnp.where(lane < DIFF_HD, dqh, 0.0)
            qd_s[rows + r0:rows + r0 + nq, :] = jnp.where(lane >= DIFF_HD, dqh, 0.0)
        _init_softmax_state(*state_refs[:3])
        _init_softmax_state(*state_refs[3:])

    qw = qw_s[...].astype(BF16)
    qpe = qpe_s[:, :ROPE].astype(BF16)
    qd = qd_s[...].astype(BF16)

    def scores(ckv, sp, dk, bias_d, mask_a):
        kt = _dot_nt(jnp.concatenate([wukt_ref[...], qw], axis=0), ckv)
        sn = kt[HEADS * NOPE:, :]
        parts = []
        for hd in range(HEADS):
            kh = kt[hd * NOPE:(hd + 1) * NOPE, :]
            r = lax.rsqrt(jnp.sum(kh * kh, axis=0, keepdims=True) * (1.0 / NOPE) + EPS)
            parts.append(sn[hd * nq:(hd + 1) * nq, :] * r)
        s = jnp.concatenate(parts, axis=0) + sp
        if mask_a is not None:
            s = s + mask_a
        sd = _dot_nt(qd, dk) + jnp.concatenate([bias_d, bias_d], axis=0)
        return s, sd

    def accumulate(state, s, sd, ckv, dv):
        ma, la, acca, md, ld, accd = state
        p, alpha, ma, la = _softmax_update(s, ma, la)
        acca = alpha * acca + _dot(p.astype(BF16), ckv)
        pd, alphad, md, ld = _softmax_update(sd, md, ld)
        accd = alphad * accd + _dot(pd.astype(BF16), dv)
        return ma, la, acca, md, ld, accd

    def cat(pages, axis):
        return jnp.concatenate([p[...].astype(BF16) for p in pages], axis=axis)

    n_chunks = pps // cpp
    s_parts, sd_parts, ckv_parts = [], [], []
    for c in range(n_chunks):
        sl = slice(c * cpp, (c + 1) * cpp)
        ckv = cat(ckv_pages[sl], 0)
        sp = _dot(qpe, cat(kpe_pages[sl], 1))
        bias = btile_ref[...] if c == n_chunks - 1 else bfar_ref[...]
        s, sd = scores(ckv, sp, cat(dk_pages[sl], 0), bias, None)
        s_parts.append(s)
        sd_parts.append(sd)
        ckv_parts.append(ckv)
    state = accumulate(tuple(r[...] for r in state_refs), jnp.concatenate(s_parts, axis=1),
                       jnp.concatenate(sd_parts, axis=1), jnp.concatenate(ckv_parts, axis=0), cat(dv_pages, 0))
    for r, val in zip(state_refs, state):
        r[...] = val

    @pl.when(j == last)
    def _():
        def padded(ref):
            x = ref[...].astype(F32)
            return jnp.concatenate([x, jnp.zeros((LANE - nq, x.shape[1]), F32)], axis=0).astype(BF16)

        sp = _dot_nt(qpe, padded(nkpe_ref)[:, :ROPE])
        nckv = padded(nckv_ref)
        s, sd = scores(nckv, sp, padded(ndk_ref), bnew_ref[...], mnew_ref[...])
        ma, la, acca, md, ld, accd = accumulate(tuple(r[...] for r in state_refs), s, sd, nckv, padded(ndv_ref))
        oa = acca / la
        lam = _lambda_value(lam_ref, lam_init)
        od = accd[:rows] / ld[:rows] - lam * (accd[rows:] / ld[rows:])
        od = _sub_norm(od, gsub_ref[...], lam_init)
        for hd in range(HEADS):
            olat_ref[:, hd * KV_RANK:(hd + 1) * KV_RANK] = oa[hd * nq:(hd + 1) * nq].astype(olat_ref.dtype)
            odiff_ref[:, hd * VD:(hd + 1) * VD] = od[hd * nq:(hd + 1) * nq].astype(odiff_ref.dtype)


def _sample_attention(layer, page_table, qw, qm, dq, nckv, nkpe, ndk, ndv, wukt, btile, bnew, mnew, bfar,
                      lam, gsub, c_ckv, c_kpe_t, c_dk, c_dv, pps, cpp, lam_init):
    db, nq, _ = qw.shape
    n_pages = page_table.shape[1]
    nj = n_pages // pps
    rows = HEADS * nq

    seq3 = lambda a: pl.BlockSpec((None,) + a.shape[1:], lambda b, j, pt: (b, 0, 0))
    const = lambda a: pl.BlockSpec(a.shape, lambda b, j, pt: (0,) * a.ndim)

    def page_specs(cache):
        return [pl.BlockSpec((None, None) + cache.shape[2:],
                             functools.partial(lambda b, j, pt, i: (layer, pt[b, j * pps + i], 0, 0), i=i))
                for i in range(pps)]

    btile_spec = pl.BlockSpec((None,) + btile.shape[1:], lambda b, j, pt: (jnp.where(j == nj - 1, 1, 0), 0, 0))
    in_specs = ([seq3(qw), seq3(qm), seq3(dq), seq3(nckv), seq3(nkpe), seq3(ndk), seq3(ndv), const(wukt),
                 btile_spec, const(bnew), const(mnew), const(bfar), const(lam), const(gsub)]
                + page_specs(c_ckv) + page_specs(c_kpe_t) + page_specs(c_dk) + page_specs(c_dv))
    out_specs = [pl.BlockSpec((None, nq, HEADS * KV_RANK), lambda b, j, pt: (b, 0, 0)),
                 pl.BlockSpec((None, nq, HEADS * VD), lambda b, j, pt: (b, 0, 0))]
    scratch = [pltpu.VMEM((rows, KV_RANK), F32), pltpu.VMEM((rows, LANE), F32),
               pltpu.VMEM((2 * rows, 2 * DIFF_HD), F32),
               pltpu.VMEM((rows, 1), F32), pltpu.VMEM((rows, 1), F32), pltpu.VMEM((rows, KV_RANK), F32),
               pltpu.VMEM((2 * rows, 1), F32), pltpu.VMEM((2 * rows, 1), F32), pltpu.VMEM((2 * rows, VD), F32)]
    return pl.pallas_call(
        functools.partial(_sample_kernel, pps=pps, cpp=cpp, lam_init=lam_init),
        grid_spec=pltpu.PrefetchScalarGridSpec(
            num_scalar_prefetch=1, grid=(db, nj), in_specs=in_specs, out_specs=out_specs,
            scratch_shapes=scratch),
        out_shape=[jax.ShapeDtypeStruct((db, nq, HEADS * KV_RANK), BF16),
                   jax.ShapeDtypeStruct((db, nq, HEADS * VD), BF16)],
        compiler_params=_cparams(("parallel", "arbitrary")),
        name="sample_attn",
    )(page_table, qw, qm, dq, nckv, nkpe, ndk, ndv, wukt, btile, bnew, mnew, bfar, lam, gsub,
      *([c_ckv] * pps), *([c_kpe_t] * pps), *([c_dk] * pps), *([c_dv] * pps))


def _merge_kernel(x_ref, om_ref, od_ref, wa_ref, wb_ref, o_ref):
    o_ref[...] = x_ref[...] + _dot(om_ref[...], wa_ref[...]) + _dot(od_ref[...], wb_ref[...])


def _merge(x, om, od, wa, wb, tm):
    n, d = x.shape
    w = om.shape[1]
    return pl.pallas_call(
        _merge_kernel,
        grid=(n // tm,),
        in_specs=[pl.BlockSpec((tm, d), lambda i: (i, 0)),
                  pl.BlockSpec((tm, w), lambda i: (i, 0)),
                  pl.BlockSpec((tm, w), lambda i: (i, 0)),
                  pl.BlockSpec(wa.shape, lambda i: (0, 0)),
                  pl.BlockSpec(wb.shape, lambda i: (0, 0))],
        out_specs=pl.BlockSpec((tm, d), lambda i: (i, 0)),
        out_shape=jax.ShapeDtypeStruct((n, d), F32),
        compiler_params=_cparams(("parallel",)),
        name="merge",
    )(x, om, od, wa, wb)


def _t5_bucket(dist):
    n = jnp.maximum(dist, 0)
    max_exact = N_BUCKETS // 2
    nf = jnp.maximum(n, 1).astype(F32)
    large = max_exact + (jnp.log(nf / max_exact) / math.log(MAX_DISTANCE / max_exact)
                         * (N_BUCKETS - max_exact)).astype(jnp.int32)
    large = jnp.minimum(large, N_BUCKETS - 1)
    return jnp.where(n < max_exact, n, large)


def _bias_table(rel_bias, dist):
    bucket = _t5_bucket(dist)[None]
    out = jnp.zeros((rel_bias.shape[1],) + dist.shape, F32)
    for k in range(N_BUCKETS):
        val = rel_bias[k].astype(F32).reshape((-1,) + (1,) * dist.ndim)
        out = jnp.where(bucket == k, val, out)
    return jnp.where(dist[None] >= 0, out, NEG)


def _rope_tables(pos):
    inv = ROPE_THETA ** (-jnp.arange(0, ROPE, 2, dtype=F32) / ROPE)
    ang = pos.astype(F32)[:, None] * inv[None, :]
    c, s = jnp.cos(ang), jnp.sin(ang)
    z = jnp.zeros((pos.shape[0], LANE - ROPE), F32)
    return jnp.concatenate([c, c, z], axis=-1), jnp.concatenate([-s, s, z], axis=-1)


def _pick_tile(n, prefs):
    for t in prefs:
        if n % t == 0:
            return t
    return n


def kernel(x_prompt, x_sample, cache_mla_ckv, cache_mla_kpe, cache_diff_k, cache_diff_v, page_table,
           g_ffn1, w1_gate, w1_up, w1_down, g_attn, w_in, g_cq, w_uq, g_qn, g_qr, g_ckv, g_kr,
           w_uk, g_kn, w_uv, g_dq, g_dk, lam_q1, lam_k1, lam_q2, lam_k2, g_sub, rel_bias, w_o,
           g_ffn2, w2_gate, w2_up, w2_down):
    batch, seq, d = x_prompt.shape
    db, nq, _ = x_sample.shape
    depth = g_ffn1.shape[0]
    n_pages = page_table.shape[1]
    page = cache_mla_ckv.shape[2]
    past = n_pages * page
    n_p = batch * seq
    n_s = db * nq
    n = n_p + n_s

    tm = _pick_tile(math.gcd(n_p, n_s), (512, 256, 128))
    tq = _pick_tile(seq, (512, 256, 128))
    pps = _pick_tile(n_pages, (16, 8, 4, 2, 1))
    cpp = pps
    chunk_keys = cpp * page

    pos = jnp.concatenate([jnp.tile(jnp.arange(seq, dtype=jnp.int32), batch),
                           jnp.tile(past + jnp.arange(nq, dtype=jnp.int32), db)])
    cos_t, sin_t = _rope_tables(pos)

    ar = jnp.arange(tq, dtype=jnp.int32)
    d0 = ar[:, None] - ar[None, :]
    bias_prompt = jnp.stack([_bias_table(rel_bias, d0), _bias_table(rel_bias, d0 + tq)], axis=1)
    far = rel_bias[_t5_bucket(jnp.int32(2 * MAX_DISTANCE))].astype(F32)
    qa = jnp.arange(nq, dtype=jnp.int32)
    rows = HEADS * nq
    d_last = chunk_keys + qa[:, None] - jnp.arange(chunk_keys, dtype=jnp.int32)[None, :]
    bias_last = _bias_table(rel_bias, d_last).reshape(rows, chunk_keys)
    i_new = jnp.arange(LANE, dtype=jnp.int32)
    d_new = jnp.where(i_new[None, :] < nq, qa[:, None] - i_new[None, :], -1)
    bias_new = _bias_table(rel_bias, d_new).reshape(rows, LANE)
    mask_new = jnp.tile(jnp.where(d_new >= 0, 0.0, NEG).astype(F32), (HEADS, 1))
    bias_far = jnp.repeat(far, nq)[:, None]
    bias_tile = jnp.stack([jnp.broadcast_to(bias_far, (rows, chunk_keys)), bias_last])
    cache_kpe_t = jnp.swapaxes(cache_mla_kpe, 2, 3)

    f = w1_gate.shape[-1]
    tf = 512
    f_pad = -(-f // tf) * tf

    def ffn_weights(wg, wu, wd):
        padc = lambda w: jnp.pad(w.astype(BF16), ((0, 0), (0, f_pad - f)))
        return padc(wg), padc(wu), jnp.pad(wd.astype(BF16), ((0, f_pad - f), (0, 0)))

    row = lambda v: v.reshape(1, -1).astype(F32)
    pad_to = lambda v, w: jnp.pad(v, ((0, 0), (0, w - v.shape[1])))

    xs = jnp.concatenate([x_prompt.reshape(n_p, d), x_sample.reshape(n_s, d)], axis=0)
    outs = [[] for _ in range(8)]
    for l in range(depth):
        lam_init = 0.8 - 0.6 * math.exp(-0.3 * l)

        wi = w_in[l]
        o_cq, o_ckv, o_kpe, o_dq, o_dk, o_dv = 0, Q_RANK, Q_RANK + KV_RANK, Q_RANK + KV_RANK + ROPE, \
            Q_RANK + KV_RANK + ROPE + HEADS * 2 * DIFF_HD, Q_RANK + KV_RANK + ROPE + HEADS * 2 * DIFF_HD + 2 * DIFF_HD
        win = jnp.concatenate([wi[:, o_cq:o_kpe], wi[:, o_dq:], wi[:, o_kpe:o_dq],
                               jnp.zeros((d, LANE - ROPE), wi.dtype)], axis=1).astype(BF16)
        wq = w_uq[l].reshape(Q_RANK, HEADS, NOPE + ROPE)
        wq_rope = jnp.pad(wq[:, :, NOPE:], ((0, 0), (0, 0), (0, LANE - ROPE)))
        wuq = jnp.concatenate([wq[:, :, :NOPE].reshape(Q_RANK, HEADS * NOPE),
                               wq_rope.reshape(Q_RANK, HEADS * LANE)], axis=1).astype(BF16)
        gqn = jnp.tile(row(g_qn[l]), (1, HEADS))
        gqr = jnp.tile(pad_to(row(g_qr[l]), LANE), (1, HEADS))
        gkr = pad_to(row(g_kr[l]), LANE)
        gdq = jnp.tile(row(g_dq[l]), (1, HEADS))
        gdk = row(g_dk[l])
        gkn = jnp.tile(row(g_kn[l]), (1, HEADS))
        wuk2 = w_uk[l].reshape(KV_RANK, HEADS * NOPE)
        wuv2 = w_uv[l].reshape(KV_RANK, HEADS * VD).astype(BF16)
        wukt = wuk2.T
        wabs = (wukt * gkn.reshape(-1, 1)).astype(BF16)
        lam = jnp.stack([lam_q1[l], lam_k1[l], lam_q2[l], lam_k2[l]]).astype(F32)
        gsub = row(g_sub[l])
        wo = w_o[l].astype(BF16)

        xs = _ffn(xs, row(g_ffn1[l]), *ffn_weights(w1_gate[l], w1_up[l], w1_down[l]), tm, tf)
        ckv, kpe, dk, dv, qmla, dq, kpeb, dkb, dvb = _proj(
            xs, row(g_attn[l]), win, row(g_cq[l]), wuq, gqn, gqr, row(g_ckv[l]), gkr, gdq, gdk,
            cos_t, sin_t, tm)

        kn, v = _kvexp(ckv, wuk2.astype(BF16), wuv2, gkn, n_p, tm)
        om_p = _mla_prompt(qmla, kn, kpeb, v, batch, seq, tq)
        od_p = _diff_prompt(far, dq, dkb, dvb, bias_prompt, lam, gsub, batch, seq, tq, lam_init)

        qm_s = qmla[n_p:]
        qw = _absorb(qm_s, wabs)
        three = lambda a: a.reshape(db, nq, a.shape[-1])
        olat, od_s = _sample_attention(
            l, page_table, three(qw), three(qm_s.astype(F32)), three(dq[n_p:].astype(F32)),
            three(ckv[n_p:]), three(kpe[n_p:]), three(dk[n_p:]), three(dv[n_p:]),
            wukt.astype(BF16), bias_tile, bias_new, mask_new, bias_far, lam, gsub,
            cache_mla_ckv, cache_kpe_t, cache_diff_k, cache_diff_v, pps, cpp, lam_init)
        om_s = _uvexp(olat.reshape(n_s, HEADS * KV_RANK), wuv2)

        om = jnp.concatenate([om_p, om_s], axis=0)
        od = jnp.concatenate([od_p, od_s.reshape(n_s, HEADS * VD)], axis=0)
        xs = _merge(xs, om, od, wo[:HEADS * VD], wo[HEADS * VD:], tm)
        xs = _ffn(xs, row(g_ffn2[l]), *ffn_weights(w2_gate[l], w2_up[l], w2_down[l]), tm, tf)

        for lst, a, w in zip(outs[:4], (ckv, kpe, dk, dv), (KV_RANK, ROPE, 2 * DIFF_HD, VD)):
            lst.append(a[:n_p, :w].reshape(batch, seq, w))
        for lst, a, w in zip(outs[4:], (ckv, kpe, dk, dv), (KV_RANK, ROPE, 2 * DIFF_HD, VD)):
            lst.append(a[n_p:, :w].reshape(db, nq, w))

    return (xs[:n_p].reshape(batch, seq, d), xs[n_p:].reshape(db, nq, d)) + tuple(jnp.stack(o) for o in outs)
```
